```python
import jax, jax.numpy as jnp
from jax import lax
import numpy as np

D_MODEL = 1024
BATCH = 4
SEQ = 8192
DEPTH = 4
DEC_BATCH = 8
DEC_SEQ = 8192
PAST_LEN = 128

N_MIXERS = 2
N_A_LAYERS = (DEPTH + 1) // 2
N_B_LAYERS = DEPTH // 2
HG_EXPAND = 128
HG_HEADS = D_MODEL // HG_EXPAND
HG_CHUNK = 64
ATT_HEADS = 16
ATT_HEAD_DIM = D_MODEL // ATT_HEADS
ATT_WINDOWS = (128, 512, 2048)
ATT_DILATIONS = (1, 4, 16)
N_ATT_GROUPS = 3
D_FF = 3584
N_EXPERTS = 8
TOP_K = 2
EPS = 1e-6

kernel_name = "hgrn2_dilated_attn_hybrid_encoder"


def rmsnorm(x, w):
    xf = x.astype(jnp.float32)
    y = xf * lax.rsqrt(jnp.mean(xf * xf, axis=-1, keepdims=True) + EPS)
    return (y * w.astype(jnp.float32)).astype(x.dtype)


def swiglu(t, w_gate, w_up, w_down):
    return (jax.nn.silu(t @ w_gate) * (t @ w_up)) @ w_down


def alibi_slopes(n):
    return jnp.asarray(2.0 ** (-8.0 * (np.arange(n) + 1) / n), dtype=jnp.float32)


def chunk_gated_recurrence(q, k, v, g):
    B, S, H, K = q.shape
    V = v.shape[-1]
    n = S // HG_CHUNK

    def chunks(t):
        return t.astype(jnp.float32).reshape(B, n, HG_CHUNK, H, t.shape[-1]).transpose(1, 0, 3, 2, 4)

    causal = jnp.tril(jnp.ones((HG_CHUNK, HG_CHUNK), bool))[:, :, None]

    def step(state, inp):
        qc, kc, vc, gc = inp
        b = jnp.cumsum(gc, axis=2)
        rel = jnp.where(causal, b[:, :, :, None, :] - b[:, :, None, :, :], -jnp.inf)
        scores = jnp.einsum('bhtk,bhsk,bhtsk->bhts', qc, kc, jnp.exp(rel))
        o = (jnp.einsum('bhts,bhsv->bhtv', scores, vc)
             + jnp.einsum('bhtk,bhkv->bhtv', qc * jnp.exp(b), state))
        b_end = b[:, :, -1:, :]
        state = (jnp.exp(b_end[:, :, 0, :, None]) * state
                 + jnp.einsum('bhsk,bhsv->bhkv', kc * jnp.exp(b_end - b), vc))
        return state, o

    state0 = jnp.zeros((B, H, K, V), jnp.float32)
    _, o = lax.scan(step, state0, (chunks(q), chunks(k), chunks(v), chunks(g)))
    return o.transpose(1, 0, 3, 2, 4).reshape(B, S, H, V)


def hgrn2_mixer(h, w_in, lb, w_norm, w_out):
    B, S, D = h.shape
    q, f_fwd, f_bwd, i_val, gate = jnp.split(h @ w_in, 5, axis=-1)
    heads = lambda t: t.reshape(B, S, HG_HEADS, HG_EXPAND)
    q = heads(jax.nn.silu(q))
    i_val = heads(i_val)

    def direction(f_raw, lb_dir, flip):
        f = heads(lb_dir + (1.0 - lb_dir) * jax.nn.sigmoid(f_raw.astype(jnp.float32)))
        args = (q, 1.0 - f, i_val, jnp.log(f))
        if flip:
            args = tuple(jnp.flip(t, axis=1) for t in args)
            return jnp.flip(chunk_gated_recurrence(*args), axis=1)
        return chunk_gated_recurrence(*args)

    o = direction(f_fwd, lb[0], False) + direction(f_bwd, lb[1], True)
    o = o * lax.rsqrt(jnp.mean(o * o, axis=-1, keepdims=True) + EPS) * w_norm.astype(jnp.float32)
    o = o.reshape(B, S, D) * jax.nn.silu(gate.astype(jnp.float32))
    return o.astype(h.dtype) @ w_out


def dilated_group_attention(q, k, v, dilation, radius, slopes):
    B, S, H, E = q.shape
    L = S // dilation
    blk = radius
    nb = -(-L // blk)
    Lp = nb * blk

    def strided(t):
        return t.reshape(B, L, dilation, H, E).transpose(0, 2, 1, 3, 4)

    qs = jnp.pad(strided(q), ((0, 0), (0, 0), (0, Lp - L), (0, 0), (0, 0))).reshape(B, dilation, nb, blk, H, E)

    def key_windows(t):
        tp = jnp.pad(strided(t), ((0, 0), (0, 0), (blk, Lp - L + blk), (0, 0), (0, 0)))
        tp = tp.reshape(B, dilation, nb + 2, blk, H, E)
        return jnp.concatenate([tp[:, :, :-2], tp[:, :, 1:-1], tp[:, :, 2:]], axis=3)

    kw = key_windows(k)
    vw = key_windows(v)
    s = jnp.einsum('bdnqhe,bdnkhe->bdnhqk', qs, kw).astype(jnp.float32) * (E ** -0.5)
    qi = jnp.arange(blk)[:, None]
    kj = jnp.arange(3 * blk)[None, :]
    rel = kj - blk - qi
    kpos = jnp.arange(nb)[:, None, None] * blk + kj[None] - blk
    valid = (jnp.abs(rel) <= radius)[None] & (kpos >= 0) & (kpos < L)
    bias = -slopes[:, None, None] * (dilation * jnp.abs(rel)).astype(jnp.float32)[None]
    s = jnp.where(valid[:, None], s + bias, -jnp.inf)
    m = jnp.max(s, axis=-1, keepdims=True)
    p = jnp.exp(s - m)
    den = jnp.sum(p, axis=-1)
    num = jnp.einsum('bdnhqk,bdnkhe->bdnqhe', p, vw.astype(jnp.float32))
    o = num / jnp.swapaxes(den, -1, -2)[..., None]
    lse = jnp.swapaxes(m[..., 0] + jnp.log(den), -1, -2)
    o = o.reshape(B, dilation, Lp, H, E)[:, :, :L].transpose(0, 2, 1, 3, 4).reshape(B, S, H, E)
    lse = lse.reshape(B, dilation, Lp, H)[:, :, :L].transpose(0, 2, 1, 3).reshape(B, S, H)
    return o, lse


def dilated_attention_mixer(h, w_in, w_out):
    B, S, D = h.shape
    proj = (h @ w_in).reshape(B, S, N_ATT_GROUPS, 3, ATT_HEADS, ATT_HEAD_DIM)
    slopes = alibi_slopes(ATT_HEADS)
    outs, lses = [], []
    for g in range(N_ATT_GROUPS):
        dil = ATT_DILATIONS[g]
        radius = ATT_WINDOWS[g] // (2 * dil)
        o, lse = dilated_group_attention(proj[:, :, g, 0], proj[:, :, g, 1], proj[:, :, g, 2], dil, radius, slopes)
        outs.append(o)
        lses.append(lse)
    wts = jax.nn.softmax(jnp.stack(lses), axis=0)
    o = jnp.einsum('gbsh,gbshe->bshe', wts, jnp.stack(outs))
    return o.reshape(B, S, D).astype(h.dtype) @ w_out


def moe_swiglu(h, w_router, w_gate, w_up, w_down):
    B, S, D = h.shape
    t = h.reshape(B * S, D)
    logits = (t @ w_router).astype(jnp.float32)
    top_val, top_idx = lax.top_k(logits, TOP_K)
    top_w = jax.nn.softmax(top_val, axis=-1)
    gates = jnp.einsum('nk,nke->ne', top_w, jax.nn.one_hot(top_idx, N_EXPERTS, dtype=jnp.float32)).astype(h.dtype)
    y = jnp.zeros_like(t)
    for e in range(N_EXPERTS):
        y = y + gates[:, e:e + 1] * swiglu(t, w_gate[e], w_up[e], w_down[e])
    return y.reshape(B, S, D)


def trunk(x, norm_mix, norm_ffn, norm_out, hg_w_in, hg_lower_bounds, hg_norm, hg_w_out,
          attn_w_in, attn_w_out, ffn_w_gate, ffn_w_up, ffn_w_down,
          moe_router, moe_w_gate, moe_w_up, moe_w_down):
    p = jax.nn.softmax(hg_lower_bounds.astype(jnp.float32), axis=0)
    lbs = jnp.cumsum(p, axis=0) - p[0]
    for i in range(DEPTH):
        a = i // N_MIXERS
        h = rmsnorm(x, norm_mix[i])
        if i % N_MIXERS == 0:
            x = x + hgrn2_mixer(h, hg_w_in[a], lbs[i], hg_norm[a], hg_w_out[a])
        else:
            x = x + dilated_attention_mixer(h, attn_w_in[a], attn_w_out[a])
        h = rmsnorm(x, norm_ffn[i])
        c = i // 2
        if i % 2 == 0:
            x = x + swiglu(h, ffn_w_gate[c], ffn_w_up[c], ffn_w_down[c])
        else:
            x = x + moe_swiglu(h, moe_router[c], moe_w_gate[c], moe_w_up[c], moe_w_down[c])
    return rmsnorm(x, norm_out)


def setup_inputs(seed: int = 0) -> dict:
    key = jax.random.key(seed)
    ks = jax.random.split(key, 18)
    D = D_MODEL
    f32 = jnp.float32
    nrm = lambda k, shape, fan_in: jax.random.normal(k, shape, f32) * fan_in ** -0.5
    return {
        "x_prompt": jax.random.normal(ks[0], (BATCH, SEQ, D), f32),
        "x_sample": jax.random.normal(ks[1], (DEC_BATCH, DEC_SEQ, D), f32),
        "norm_mix": 1.0 + 0.02 * jax.random.normal(ks[2], (DEPTH, D), f32),
        "norm_ffn": 1.0 + 0.02 * jax.random.normal(ks[3], (DEPTH, D), f32),
        "norm_out": 1.0 + 0.02 * jax.random.normal(ks[4], (D,), f32),
        "hg_w_in": nrm(ks[5], (N_A_LAYERS, D, 5 * D), D),
        "hg_lower_bounds": 0.1 * jax.random.normal(ks[6], (DEPTH, 2, D), f32),
        "hg_norm": 1.0 + 0.02 * jax.random.normal(ks[7], (N_A_LAYERS, HG_EXPAND), f32),
        "hg_w_out": nrm(ks[8], (N_A_LAYERS, D, D), D),
        "attn_w_in": nrm(ks[9], (N_B_LAYERS, D, 3 * N_ATT_GROUPS * D), D),
        "attn_w_out": nrm(ks[10], (N_B_LAYERS, D, D), D),
        "ffn_w_gate": nrm(ks[11], (N_A_LAYERS, D, D_FF), D),
        "ffn_w_up": nrm(ks[12], (N_A_LAYERS, D, D_FF), D),
        "ffn_w_down": nrm(ks[13], (N_A_LAYERS, D_FF, D), D_FF),
        "moe_router": nrm(ks[14], (N_B_LAYERS, D, N_EXPERTS), D),
        "moe_w_gate": nrm(ks[15], (N_B_LAYERS, N_EXPERTS, D, D_FF), D),
        "moe_w_up": nrm(ks[16], (N_B_LAYERS, N_EXPERTS, D, D_FF), D),
        "moe_w_down": nrm(ks[17], (N_B_LAYERS, N_EXPERTS, D_FF, D), D_FF),
    }


def reference(x_prompt, x_sample, norm_mix, norm_ffn, norm_out, hg_w_in, hg_lower_bounds, hg_norm, hg_w_out,
              attn_w_in, attn_w_out, ffn_w_gate, ffn_w_up, ffn_w_down,
              moe_router, moe_w_gate, moe_w_up, moe_w_down):
    y_prompt = trunk(x_prompt, norm_mix, norm_ffn, norm_out, hg_w_in, hg_lower_bounds, hg_norm, hg_w_out,
                     attn_w_in, attn_w_out, ffn_w_gate, ffn_w_up, ffn_w_down,
                     moe_router, moe_w_gate, moe_w_up, moe_w_down)
    y_sample = trunk(x_sample, norm_mix, norm_ffn, norm_out, hg_w_in, hg_lower_bounds, hg_norm, hg_w_out,
                     attn_w_in, attn_w_out, ffn_w_gate, ffn_w_up, ffn_w_down,
                     moe_router, moe_w_gate, moe_w_up, moe_w_down)
    return (y_prompt, y_sample)
```

```python
import functools

import jax
import jax.numpy as jnp
import numpy as np
from jax import lax
from jax.experimental import pallas as pl
from jax.experimental.pallas import tpu as pltpu

F32 = jnp.float32
BF16 = jnp.bfloat16

EPS = 1e-6
HG_HEAD_DIM = 128
HG_CHUNK = 64
HG_SUB = 16
HG_EXP_CLAMP = 80.0
ATT_HEADS = 16
ATT_HEAD_DIM = 64
ATT_RADIUS = 64
ATT_DILATIONS = (1, 4, 16)
ATT_QB = 128
ATT_KW = ATT_QB + 2 * ATT_RADIUS
N_EXPERTS = 8
LANES = 128
VMEM_LIMIT = 56 * 1024 * 1024


def _params(sem):
    return pltpu.CompilerParams(dimension_semantics=sem, vmem_limit_bytes=VMEM_LIMIT)


def _rms(x, w):
    return (x * lax.rsqrt(jnp.mean(x * x, axis=-1, keepdims=True) + EPS)) * w


def _sigmoid(x):
    return 1.0 / (1.0 + jnp.exp(-x))


def _dot(a, b):
    return jnp.dot(a, b, preferred_element_type=F32)


def _dot_nt(a, b):
    return lax.dot_general(a, b, (((1,), (1,)), ((), ())), preferred_element_type=F32)


def _dot_tn(a, b):
    return lax.dot_general(a, b, (((0,), (0,)), ((), ())), preferred_element_type=F32)


def _norm_proj_kernel(x_ref, nw_ref, w_ref, o_ref, h_ref, *xs_ref, dil, tm):
    rows = tm // dil

    @pl.when(pl.program_id(2) == 0)
    def _():
        nw = nw_ref[...]
        if dil == 1:
            h_ref[...] = _rms(x_ref[...], nw).astype(BF16)
        else:
            xs = xs_ref[0]
            n_chunks = xs.shape[0]
            for c in range(n_chunks):
                xs[c] = x_ref[:, c * LANES:(c + 1) * LANES]
            for r in range(dil):
                xr = jnp.concatenate(
                    [xs[c, pl.ds(r, rows, stride=dil), :] for c in range(n_chunks)], axis=1)
                h_ref[r * rows:(r + 1) * rows, :] = _rms(xr, nw).astype(BF16)

    acc = _dot(h_ref[...], w_ref[...])
    if dil == 1:
        o_ref[0] = acc.astype(o_ref.dtype)
    else:
        for r in range(dil):
            o_ref[r] = acc[r * rows:(r + 1) * rows].astype(o_ref.dtype)


def _norm_proj(x, nw, w, *, dil=1, tm=1024, tn=1024, out_dtype=BF16):
    B, S, D = x.shape
    N = w.shape[1]
    tm = min(tm, S)
    tn = min(tn, N)
    kern = functools.partial(_norm_proj_kernel, dil=dil, tm=tm)
    return pl.pallas_call(
        kern,
        grid=(B, S // tm, N // tn),
        in_specs=[
            pl.BlockSpec((None, tm, D), lambda b, i, j: (b, i, 0)),
            pl.BlockSpec((1, D), lambda b, i, j: (0, 0)),
            pl.BlockSpec((D, tn), lambda b, i, j: (0, j)),
        ],
        out_specs=pl.BlockSpec((None, dil, tm // dil, tn), lambda b, i, j: (b, 0, i, j)),
        out_shape=jax.ShapeDtypeStruct((B, dil, S // dil, N), out_dtype),
        scratch_shapes=[pltpu.VMEM((tm, D), BF16)]
        + ([pltpu.VMEM((D // LANES, tm, LANES), F32)] if dil > 1 else []),
        compiler_params=_params(("parallel", "parallel", "arbitrary")),
        name="norm_proj",
    )(x, nw, w)


def _hgrn_chunk(q_ref, f_ref, v_ref, o_ref, st_ref, r0, lb, tri, rev):
    C, SUB = HG_CHUNK, HG_SUB
    q = q_ref[pl.ds(r0, C), :].astype(F32)
    q = q * _sigmoid(q)
    f = lb + (1.0 - lb) * _sigmoid(f_ref[pl.ds(r0, C), :].astype(F32))
    g = jnp.log(f)
    k = 1.0 - f
    v = v_ref[pl.ds(r0, C), :]

    g1 = g.astype(BF16)
    d1 = g - g1.astype(F32)
    g2 = d1.astype(BF16)
    g3 = (d1 - g2.astype(F32)).astype(BF16)
    b = _dot(tri, g1) + _dot(tri, g2) + _dot(tri, g3)
    b_tot = b[0:1, :] if rev else b[C - 1:C, :]

    st = st_ref[...]
    o_inter = _dot_nt((q * jnp.exp(b)).astype(BF16), st.astype(BF16))
    kk = (k * jnp.exp(b_tot - b)).astype(BF16)
    st_ref[...] = st * jnp.exp(b_tot) + _dot_tn(v, kk)

    outs = []
    for i in range(C // SUB):
        lo, hi = i * SUB, (i + 1) * SUB
        if rev:
            klo, khi, mid = lo, C, lo + SUB // 2
        else:
            klo, khi, mid = 0, hi, lo + SUB // 2 - 1
        mu = b[mid:mid + 1, :]
        qi = (q[lo:hi] * jnp.exp(jnp.minimum(b[lo:hi] - mu, HG_EXP_CLAMP))).astype(BF16)
        ki = (k[klo:khi] * jnp.exp(jnp.minimum(mu - b[klo:khi], HG_EXP_CLAMP))).astype(BF16)
        a = _dot_nt(qi, ki)
        t_idx = lo + lax.broadcasted_iota(jnp.int32, a.shape, 0)
        s_idx = klo + lax.broadcasted_iota(jnp.int32, a.shape, 1)
        a = jnp.where(s_idx >= t_idx if rev else s_idx <= t_idx, a, 0.0)
        outs.append(_dot(a.astype(BF16), v[klo:khi]))
    o_ref[pl.ds(r0, C), :] = o_inter + jnp.concatenate(outs, axis=0)


def _hgrn_kernel(qf_ref, ff_ref, vf_ref, qb_ref, fb_ref, vb_ref, lb_ref, of_ref, ob_ref,
                 sf_ref, sb_ref, *, T):
    C = HG_CHUNK

    @pl.when(pl.program_id(2) == 0)
    def _():
        sf_ref[...] = jnp.zeros_like(sf_ref)
        sb_ref[...] = jnp.zeros_like(sb_ref)

    row = lax.broadcasted_iota(jnp.int32, (C, C), 0)
    col = lax.broadcasted_iota(jnp.int32, (C, C), 1)
    tri_f = jnp.where(col <= row, 1.0, 0.0).astype(BF16)
    tri_b = jnp.where(col >= row, 1.0, 0.0).astype(BF16)
    lbf = lb_ref[0:1, :]
    lbb = lb_ref[1:2, :]
    n_chunks = T // C

    def body(c, carry):
        r_f = pl.multiple_of(c * C, C)
        _hgrn_chunk(qf_ref, ff_ref, vf_ref, of_ref, sf_ref, r_f, lbf, tri_f, False)
        r_b = pl.multiple_of((n_chunks - 1 - c) * C, C)
        _hgrn_chunk(qb_ref, fb_ref, vb_ref, ob_ref, sb_ref, r_b, lbb, tri_b, True)
        return carry

    lax.fori_loop(0, n_chunks, body, 0)


def _hgrn_recurrence(proj, lb, *, T=1024):
    B, S, D5 = proj.shape
    D = D5 // 5
    H = D // HG_HEAD_DIM
    T = min(T, S)
    nT = S // T
    blk = (None, T, HG_HEAD_DIM)
    fwd = lambda off: pl.BlockSpec(blk, lambda b, h, t: (b, t, off + h))
    bwd = lambda off: pl.BlockSpec(blk, lambda b, h, t: (b, nT - 1 - t, off + h))
    out_shape = jax.ShapeDtypeStruct((B, S, D), F32)
    return pl.pallas_call(
        functools.partial(_hgrn_kernel, T=T),
        grid=(B, H, nT),
        in_specs=[fwd(0), fwd(H), fwd(3 * H), bwd(0), bwd(2 * H), bwd(3 * H),
                  pl.BlockSpec((2, HG_HEAD_DIM), lambda b, h, t: (0, h))],
        out_specs=[pl.BlockSpec(blk, lambda b, h, t: (b, t, h)),
                   pl.BlockSpec(blk, lambda b, h, t: (b, nT - 1 - t, h))],
        out_shape=[out_shape, out_shape],
        scratch_shapes=[pltpu.VMEM((HG_HEAD_DIM, HG_HEAD_DIM), F32),
                        pltpu.VMEM((HG_HEAD_DIM, HG_HEAD_DIM), F32)],
        compiler_params=_params(("parallel", "parallel", "arbitrary")),
        name="hgrn_recurrence",
    )(proj, proj, proj, proj, proj, proj, lb)


def _hgrn_out_kernel(of_ref, ob_ref, gate_ref, x_ref, wn_ref, wo_ref, out_ref):
    o = of_ref[...] + ob_ref[...]
    D = o.shape[-1]
    parts = []
    for h in range(D // HG_HEAD_DIM):
        oh = o[:, h * HG_HEAD_DIM:(h + 1) * HG_HEAD_DIM]
        parts.append(oh * lax.rsqrt(jnp.mean(oh * oh, axis=-1, keepdims=True) + EPS))
    gate = gate_ref[...].astype(F32)
    y = (jnp.concatenate(parts, axis=-1) * wn_ref[...]) * (gate * _sigmoid(gate))
    out_ref[...] = x_ref[...] + _dot(y.astype(BF16), wo_ref[...])


def _hgrn_out(o_f, o_b, proj, x, wn_tiled, w_out, *, tm=512):
    B, S, D = x.shape
    tm = min(tm, S)
    row = pl.BlockSpec((None, tm, D), lambda b, i: (b, i, 0))
    return pl.pallas_call(
        _hgrn_out_kernel,
        grid=(B, S // tm),
        in_specs=[row, row,
                  pl.BlockSpec((None, tm, D), lambda b, i: (b, i, 4)),
                  row,
                  pl.BlockSpec((1, D), lambda b, i: (0, 0)),
                  pl.BlockSpec((D, D), lambda b, i: (0, 0))],
        out_specs=row,
        out_shape=jax.ShapeDtypeStruct((B, S, D), F32),
        compiler_params=_params(("parallel", "parallel")),
        name="hgrn_out",
    )(o_f, o_b, proj, x, wn_tiled, w_out)


def _ffn_kernel(*refs, has_gate, final_norm, tf):
    refs = list(refs)
    x_ref, nw_ref = refs[0], refs[1]
    pos = 2
    gate_ref = None
    if has_gate:
        gate_ref = refs[pos]
        pos += 1
    wg_ref, wu_ref, wd_ref = refs[pos:pos + 3]
    pos += 3
    fnw_ref = None
    if final_norm:
        fnw_ref = refs[pos]
        pos += 1
    o_ref, h_ref, acc_ref = refs[pos:pos + 3]

    e, j = pl.program_id(2), pl.program_id(3)
    first = jnp.logical_and(e == 0, j == 0)
    last = jnp.logical_and(e == pl.num_programs(2) - 1, j == pl.num_programs(3) - 1)

    @pl.when(first)
    def _():
        h_ref[...] = _rms(x_ref[...], nw_ref[...]).astype(BF16)
        acc_ref[...] = jnp.zeros_like(acc_ref)

    h = h_ref[...]
    g = _dot(h, wg_ref[...])
    u = _dot(h, wu_ref[...])
    a = (g * _sigmoid(g)) * u
    if has_gate:
        a = a * jnp.concatenate([gate_ref[...]] * (tf // LANES), axis=1)
    acc_ref[...] += _dot(a.astype(BF16), wd_ref[...])

    @pl.when(last)
    def _():
        y = x_ref[...] + acc_ref[...]
        if final_norm:
            y = _rms(y, fnw_ref[...])
        o_ref[...] = y


def _ffn(x, nw, w_gate, w_up, w_down, *, gates=None, final_nw=None, tm=1024, tf=512):
    B, S, D = x.shape
    E, _, F = w_gate.shape
    tm = min(tm, S)
    row = pl.BlockSpec((None, tm, D), lambda b, i, e, j: (b, i, 0))
    vec = pl.BlockSpec((1, D), lambda b, i, e, j: (0, 0))
    in_specs = [row, vec]
    args = [x, nw]
    if gates is not None:
        in_specs.append(pl.BlockSpec((None, tm, LANES), lambda b, i, e, j: (b, i, e)))
        args.append(gates)
    in_specs += [pl.BlockSpec((None, D, tf), lambda b, i, e, j: (e, 0, j)),
                 pl.BlockSpec((None, D, tf), lambda b, i, e, j: (e, 0, j)),
                 pl.BlockSpec((None, tf, D), lambda b, i, e, j: (e, j, 0))]
    args += [w_gate, w_up, w_down]
    if final_nw is not None:
        in_specs.append(vec)
        args.append(final_nw)
    kern = functools.partial(_ffn_kernel, has_gate=gates is not None,
                             final_norm=final_nw is not None, tf=tf)
    return pl.pallas_call(
        kern,
        grid=(B, S // tm, E, F // tf),
        in_specs=in_specs,
        out_specs=row,
        out_shape=jax.ShapeDtypeStruct((B, S, D), F32),
        scratch_shapes=[pltpu.VMEM((tm, D), BF16), pltpu.VMEM((tm, D), F32)],
        compiler_params=_params(("parallel", "parallel", "arbitrary", "arbitrary")),
        name="swiglu_ffn",
    )(*args)


def _router_kernel(x_ref, nw_ref, wr_ref, o_ref):
    h = _rms(x_ref[...], nw_ref[...])
    logits = [jnp.sum(h * wr_ref[e:e + 1, :], axis=-1, keepdims=True) for e in range(N_EXPERTS)]

    def top(vals):
        m = functools.reduce(jnp.maximum, vals)
        idx = functools.reduce(
            jnp.minimum, [jnp.where(v == m, e, N_EXPERTS) for e, v in enumerate(vals)])
        return m, idx

    m1, i1 = top(logits)
    m2, i2 = top([jnp.where(i1 == e, -jnp.inf, v) for e, v in enumerate(logits)])
    e2 = jnp.exp(m2 - m1)
    w1 = 1.0 / (1.0 + e2)
    w2 = e2 / (1.0 + e2)
    tm = h.shape[0]
    for e in range(N_EXPERTS):
        gate = jnp.where(i1 == e, w1, 0.0) + jnp.where(i2 == e, w2, 0.0)
        o_ref[:, e * LANES:(e + 1) * LANES] = jnp.broadcast_to(gate, (tm, LANES))


def _router(x, nw, w_router_t, *, tm=512):
    B, S, D = x.shape
    tm = min(tm, S)
    return pl.pallas_call(
        _router_kernel,
        grid=(B, S // tm),
        in_specs=[pl.BlockSpec((None, tm, D), lambda b, i: (b, i, 0)),
                  pl.BlockSpec((1, D), lambda b, i: (0, 0)),
                  pl.BlockSpec((N_EXPERTS, D), lambda b, i: (0, 0))],
        out_specs=pl.BlockSpec((None, tm, N_EXPERTS * LANES), lambda b, i: (b, i, 0)),
        out_shape=jax.ShapeDtypeStruct((B, S, N_EXPERTS * LANES), F32),
        compiler_params=_params(("parallel", "parallel")),
        name="moe_router",
    )(x, nw, w_router_t)


def _attn_kernel(q_ref, k_ref, v_ref, kp_ref, kn_ref, vp_ref, vn_ref, o_ref, lse_ref,
                 kw_ref, vw_ref, *, dil, L, TQ, slopes):
    R, QB, KW = ATT_RADIUS, ATT_QB, ATT_KW
    n = pl.program_id(2)
    kw_ref[0:R, :] = kp_ref[...]
    kw_ref[R:R + TQ, :] = k_ref[...]
    kw_ref[R + TQ:, :] = kn_ref[...]
    vw_ref[0:R, :] = vp_ref[...]
    vw_ref[R:R + TQ, :] = v_ref[...]
    vw_ref[R + TQ:, :] = vn_ref[...]

    qrow = lax.broadcasted_iota(jnp.int32, (QB, KW), 0)
    kcol = lax.broadcasted_iota(jnp.int32, (QB, KW), 1)
    arel = jnp.abs(kcol - R - qrow)
    band = arel <= R
    nbias = (-float(dil)) * arel.astype(F32)
    lane = lax.broadcasted_iota(jnp.int32, (QB, LANES), 1)
    lo_half = lane < ATT_HEAD_DIM

    def sub(i, carry):
        r0 = pl.multiple_of(i * QB, QB)
        kpos = n * TQ + i * QB - R + kcol
        valid = band & (kpos >= 0) & (kpos < L)
        lse_tile = jnp.zeros((QB, LANES), F32)
        for p in range(ATT_HEADS // 2):
            cs = slice(p * LANES, (p + 1) * LANES)
            q2 = q_ref[pl.ds(r0, QB), cs] * 0.125
            k2 = kw_ref[pl.ds(r0, KW), cs]
            v2 = vw_ref[pl.ds(r0, KW), cs]
            outs = []
            for hh in range(2):
                h = 2 * p + hh
                qm = jnp.where(lo_half if hh == 0 else jnp.logical_not(lo_half), q2, 0.0)
                s = _dot_nt(qm.astype(BF16), k2) + slopes[h] * nbias
                s = jnp.where(valid, s, -jnp.inf)
                m = jnp.max(s, axis=-1, keepdims=True)
                pe = jnp.exp(s - m)
                den = jnp.sum(pe, axis=-1, keepdims=True)
                outs.append(_dot(pe.astype(BF16), v2) / den)
                lse_tile = jnp.where(lane == h, m + jnp.log(den), lse_tile)
            o_ref[pl.ds(r0, QB), cs] = jnp.where(lo_half, outs[0], outs[1])
        lse_ref[pl.ds(r0, QB), :] = lse_tile
        return carry

    lax.fori_loop(0, TQ // QB, sub, 0)


def _attn_group(qkv, dil, slopes, *, TQ=512):
    B, _, L, D3 = qkv.shape
    D = D3 // 3
    R = ATT_RADIUS
    TQ = min(TQ, L)
    nq = L // TQ
    hb = TQ // R
    last_hb = L // R - 1
    main = lambda c: pl.BlockSpec((None, None, TQ, D), lambda b, r, n: (b, r, n, c))
    prev = lambda c: pl.BlockSpec((None, None, R, D),
                                  lambda b, r, n: (b, r, jnp.maximum(n * hb - 1, 0), c))
    nxt = lambda c: pl.BlockSpec((None, None, R, D),
                                 lambda b, r, n: (b, r, jnp.minimum((n + 1) * hb, last_hb), c))
    kern = functools.partial(_attn_kernel, dil=dil, L=L, TQ=TQ, slopes=slopes)
    return pl.pallas_call(
        kern,
        grid=(B, dil, nq),
        in_specs=[main(0), main(1), main(2), prev(1), nxt(1), prev(2), nxt(2)],
        out_specs=[pl.BlockSpec((None, None, TQ, D), lambda b, r, n: (b, r, n, 0)),
                   pl.BlockSpec((None, None, TQ, LANES), lambda b, r, n: (b, r, n, 0))],
        out_shape=[jax.ShapeDtypeStruct((B, dil, L, D), F32),
                   jax.ShapeDtypeStruct((B, dil, L, LANES), F32)],
        scratch_shapes=[pltpu.VMEM((TQ + 2 * R, D), BF16), pltpu.VMEM((TQ + 2 * R, D), BF16)],
        compiler_params=_params(("parallel", "parallel", "parallel")),
        name="dilated_attn",
    )(qkv, qkv, qkv, qkv, qkv, qkv, qkv)


def _attn_merge_kernel(o0_ref, o1_ref, o2_ref, l0_ref, l1_ref, l2_ref, x_ref, ex_ref, wo_ref,
                       out_ref, s1_ref, s2_ref, t1_ref, t2_ref, *, tm):
    n_chunks = s1_ref.shape[0]

    def interleave(src_ref, lse_src_ref, dst_ref, lse_dst_ref, dil):
        rows = tm // dil
        for r in range(dil):
            lse_dst_ref[pl.ds(r, rows, stride=dil), :] = lse_src_ref[r]
            for c in range(n_chunks):
                dst_ref[c, pl.ds(r, rows, stride=dil), :] = src_ref[r, :, c * LANES:(c + 1) * LANES]
        return jnp.concatenate([dst_ref[c] for c in range(n_chunks)], axis=1)

    o1 = interleave(o1_ref, l1_ref, s1_ref, t1_ref, ATT_DILATIONS[1])
    o2 = interleave(o2_ref, l2_ref, s2_ref, t2_ref, ATT_DILATIONS[2])
    la, lb, lc = l0_ref[0], t1_ref[...], t2_ref[...]
    m = jnp.maximum(jnp.maximum(la, lb), lc)
    ea, eb, ec = jnp.exp(la - m), jnp.exp(lb - m), jnp.exp(lc - m)
    den = ea + eb + ec
    ex = ex_ref[...]

    def expand(w):
        hi = w.astype(BF16)
        lo = (w - hi.astype(F32)).astype(BF16)
        return _dot(hi, ex) + _dot(lo, ex)

    o = expand(ea / den) * o0_ref[0] + expand(eb / den) * o1 + expand(ec / den) * o2
    out_ref[...] = x_ref[...] + _dot(o.astype(BF16), wo_ref[...])


def _attn_merge(outs, lses, x, expand_mat, w_out, *, tm=512):
    B, S, D = x.shape
    tm = min(tm, S)
    d0, d1, d2 = ATT_DILATIONS
    grp = lambda d, w: pl.BlockSpec((None, d, tm // d, w), lambda b, i: (b, 0, i, 0))
    row = pl.BlockSpec((None, tm, D), lambda b, i: (b, i, 0))
    return pl.pallas_call(
        functools.partial(_attn_merge_kernel, tm=tm),
        grid=(B, S // tm),
        in_specs=[grp(d0, D), grp(d1, D), grp(d2, D), grp(d0, LANES), grp(d1, LANES),
                  grp(d2, LANES), row,
                  pl.BlockSpec((LANES, D), lambda b, i: (0, 0)),
                  pl.BlockSpec((D, D), lambda b, i: (0, 0))],
        out_specs=row,
        out_shape=jax.ShapeDtypeStruct((B, S, D), F32),
        scratch_shapes=[pltpu.VMEM((D // LANES, tm, LANES), F32),
                        pltpu.VMEM((D // LANES, tm, LANES), F32),
                        pltpu.VMEM((tm, LANES), F32), pltpu.VMEM((tm, LANES), F32)],
        compiler_params=_params(("parallel", "parallel")),
        name="attn_merge",
    )(*outs, *lses, x, expand_mat, w_out)


def _alibi_slopes(n):
    return tuple(float(s) for s in np.asarray(2.0 ** (-8.0 * (np.arange(n) + 1) / n), np.float32))


def _trunk(x, p):
    depth = p["norm_mix"].shape[0]
    D = x.shape[-1]
    slopes = _alibi_slopes(ATT_HEADS)
    for i in range(depth):
        a = i // 2
        nw_mix = p["norm_mix"][i][None, :]
        nw_ffn = p["norm_ffn"][i][None, :]
        if i % 2 == 0:
            proj = _norm_proj(x, nw_mix, p["hg_w_in"][a])[:, 0]
            o_f, o_b = _hgrn_recurrence(proj, p["lbs"][i])
            x = _hgrn_out(o_f, o_b, proj, x, p["hg_norm_tiled"][a], p["hg_w_out"][a])
            x = _ffn(x, nw_ffn, p["ffn_w_gate"][a][None], p["ffn_w_up"][a][None],
                     p["ffn_w_down"][a][None])
        else:
            outs, lses = [], []
            for g, dil in enumerate(ATT_DILATIONS):
                w_g = p["attn_w_in"][a][:, 3 * D * g:3 * D * (g + 1)]
                qkv = _norm_proj(x, nw_mix, w_g, dil=dil)
                o_g, lse_g = _attn_group(qkv, dil, slopes)
                outs.append(o_g)
                lses.append(lse_g)
            x = _attn_merge(outs, lses, x, p["expand_mat"], p["attn_w_out"][a])
            gates = _router(x, nw_ffn, p["moe_router_t"][a])
            final_nw = p["norm_out"][None, :] if i == depth - 1 else None
            x = _ffn(x, nw_ffn, p["moe_w_gate"][a], p["moe_w_up"][a], p["moe_w_down"][a],
                     gates=gates, final_nw=final_nw)
    return x


def kernel(x_prompt, x_sample, norm_mix, norm_ffn, norm_out, hg_w_in, hg_lower_bounds, hg_norm,
           hg_w_out, attn_w_in, attn_w_out, ffn_w_gate, ffn_w_up, ffn_w_down, moe_router,
           moe_w_gate, moe_w_up, moe_w_down):
    D = x_prompt.shape[-1]
    assert norm_mix.shape[0] % 2 == 0, "trunk ends on an attention/MoE layer"
    pr = jax.nn.softmax(hg_lower_bounds.astype(F32), axis=0)
    head_of_lane = np.arange(D) // ATT_HEAD_DIM
    expand_mat = jnp.asarray(np.arange(LANES)[:, None] == head_of_lane[None, :], BF16)
    p = dict(
        norm_mix=norm_mix, norm_ffn=norm_ffn, norm_out=norm_out,
        lbs=jnp.cumsum(pr, axis=0) - pr[0],
        hg_w_in=hg_w_in.astype(BF16), hg_w_out=hg_w_out.astype(BF16),
        hg_norm_tiled=jnp.tile(hg_norm.astype(F32), (1, D // HG_HEAD_DIM))[:, None, :],
        attn_w_in=attn_w_in.astype(BF16), attn_w_out=attn_w_out.astype(BF16),
        ffn_w_gate=ffn_w_gate.astype(BF16), ffn_w_up=ffn_w_up.astype(BF16),
        ffn_w_down=ffn_w_down.astype(BF16),
        moe_router_t=jnp.swapaxes(moe_router, 1, 2).astype(F32),
        moe_w_gate=moe_w_gate.astype(BF16), moe_w_up=moe_w_up.astype(BF16),
        moe_w_down=moe_w_down.astype(BF16),
        expand_mat=expand_mat,
    )
    return (_trunk(x_prompt, p), _trunk(x_sample, p))
```

```python
import functools

import jax
import jax.numpy as jnp
import numpy as np
from jax import lax
from jax.experimental import pallas as pl
from jax.experimental.pallas import tpu as pltpu

F32 = jnp.float32
BF16 = jnp.bfloat16

EPS = 1e-6
HG_HEAD_DIM = 128
HG_CHUNK = 64
HG_SUB = 16
HG_EXP_CLAMP = 80.0
HG_UNROLL = 8
ATT_HEADS = 16
ATT_HEAD_DIM = 64
ATT_RADIUS = 64
ATT_DILATIONS = (1, 4, 16)
ATT_QB = 128
ATT_KW = ATT_QB + 2 * ATT_RADIUS
N_EXPERTS = 8
LANES = 128
VMEM_LIMIT = 56 * 1024 * 1024


def _params(sem):
    return pltpu.CompilerParams(dimension_semantics=sem, vmem_limit_bytes=VMEM_LIMIT)


def _rms(x, w):
    return (x * lax.rsqrt(jnp.mean(x * x, axis=-1, keepdims=True) + EPS)) * w


def _sigmoid(x):
    return 1.0 / (1.0 + jnp.exp(-x))


def _dot(a, b):
    return jnp.dot(a, b, preferred_element_type=F32)


def _dot_nt(a, b):
    return lax.dot_general(a, b, (((1,), (1,)), ((), ())), preferred_element_type=F32)


def _dot_tn(a, b):
    return lax.dot_general(a, b, (((0,), (0,)), ((), ())), preferred_element_type=F32)


def _norm_proj_kernel(x_ref, nw_ref, w_ref, o_ref, h_ref, *xs_ref, dil, tm):
    rows = tm // dil

    @pl.when(pl.program_id(2) == 0)
    def _():
        nw = nw_ref[...]
        if dil == 1:
            h_ref[...] = _rms(x_ref[...], nw).astype(BF16)
        else:
            xs = xs_ref[0]
            n_chunks = xs.shape[0]
            for c in range(n_chunks):
                xs[c] = x_ref[:, c * LANES:(c + 1) * LANES]
            for r in range(dil):
                xr = jnp.concatenate(
                    [xs[c, pl.ds(r, rows, stride=dil), :] for c in range(n_chunks)], axis=1)
                h_ref[r * rows:(r + 1) * rows, :] = _rms(xr, nw).astype(BF16)

    acc = _dot(h_ref[...], w_ref[...])
    if dil == 1:
        o_ref[0] = acc.astype(o_ref.dtype)
    else:
        for r in range(dil):
            o_ref[r] = acc[r * rows:(r + 1) * rows].astype(o_ref.dtype)


def _norm_proj(x, nw, w, *, dil=1, tm=1024, tn=1024, out_dtype=BF16):
    B, S, D = x.shape
    N = w.shape[1]
    tm = min(tm, S)
    tn = min(tn, N)
    kern = functools.partial(_norm_proj_kernel, dil=dil, tm=tm)
    return pl.pallas_call(
        kern,
        grid=(B, S // tm, N // tn),
        in_specs=[
            pl.BlockSpec((None, tm, D), lambda b, i, j: (b, i, 0)),
            pl.BlockSpec((1, D), lambda b, i, j: (0, 0)),
            pl.BlockSpec((D, tn), lambda b, i, j: (0, j)),
        ],
        out_specs=pl.BlockSpec((None, dil, tm // dil, tn), lambda b, i, j: (b, 0, i, j)),
        out_shape=jax.ShapeDtypeStruct((B, dil, S // dil, N), out_dtype),
        scratch_shapes=[pltpu.VMEM((tm, D), BF16)]
        + ([pltpu.VMEM((D // LANES, tm, LANES), F32)] if dil > 1 else []),
        compiler_params=_params(("parallel", "parallel", "arbitrary")),
        name="norm_proj",
    )(x, nw, w)


def _hgrn_gates(q, f_raw, lb, tri, rev):
    q = q.astype(F32)
    q = q * _sigmoid(q)
    f = lb + (1.0 - lb) * _sigmoid(f_raw.astype(F32))
    g = jnp.log(f)
    g1 = g.astype(BF16)
    d1 = g - g1.astype(F32)
    g2 = d1.astype(BF16)
    g3 = (d1 - g2.astype(F32)).astype(BF16)
    b = _dot(tri, g1) + _dot(tri, g2) + _dot(tri, g3)
    return q, 1.0 - f, b


def _hgrn_scores(q, k, v, b, rev):
    C, SUB = HG_CHUNK, HG_SUB
    b_tot = b[0:1, :] if rev else b[C - 1:C, :]
    q_inter = (q * jnp.exp(b)).astype(BF16)
    kk = (k * jnp.exp(b_tot - b)).astype(BF16)
    st_inc = _dot_tn(v, kk)
    scores = []
    for i in range(C // SUB):
        lo, hi = i * SUB, (i + 1) * SUB
        if rev:
            klo, khi, mid = lo, C, lo + SUB // 2
        else:
            klo, khi, mid = 0, hi, lo + SUB // 2 - 1
        mu = b[mid:mid + 1, :]
        qi = (q[lo:hi] * jnp.exp(jnp.minimum(b[lo:hi] - mu, HG_EXP_CLAMP))).astype(BF16)
        ki = (k[klo:khi] * jnp.exp(jnp.minimum(mu - b[klo:khi], HG_EXP_CLAMP))).astype(BF16)
        scores.append(_dot_nt(qi, ki))
    return q_inter, jnp.exp(b_tot), st_inc, scores


def _hgrn_outputs(scores, v, q_inter, st, rev):
    C, SUB = HG_CHUNK, HG_SUB
    outs = []
    for i, a in enumerate(scores):
        lo = i * SUB
        klo, khi = (lo, C) if rev else (0, lo + SUB)
        t_idx = lo + lax.broadcasted_iota(jnp.int32, a.shape, 0)
        s_idx = klo + lax.broadcasted_iota(jnp.int32, a.shape, 1)
        a = jnp.where(s_idx >= t_idx if rev else s_idx <= t_idx, a, 0.0)
        outs.append(_dot(a.astype(BF16), v[klo:khi]))
    return _dot_nt(q_inter, st.astype(BF16)) + jnp.concatenate(outs, axis=0)


def _hgrn_kernel(qf_ref, ff_ref, vf_ref, qb_ref, fb_ref, vb_ref, lb_ref, of_ref, ob_ref,
                 sf_ref, sb_ref, *, T):
    C, U = HG_CHUNK, HG_UNROLL
    span = C * U

    @pl.when(pl.program_id(2) == 0)
    def _():
        sf_ref[...] = jnp.zeros_like(sf_ref)
        sb_ref[...] = jnp.zeros_like(sb_ref)

    row = lax.broadcasted_iota(jnp.int32, (C, C), 0)
    col = lax.broadcasted_iota(jnp.int32, (C, C), 1)
    tri_f = jnp.where(col <= row, 1.0, 0.0).astype(BF16)
    tri_b = jnp.where(col >= row, 1.0, 0.0).astype(BF16)
    lbf = lb_ref[0:1, :]
    lbb = lb_ref[1:2, :]
    n_spans = T // span

    def body(it, carry):
        r_f = pl.multiple_of(it * span, span)
        r_b = pl.multiple_of((n_spans - 1 - it) * span, span)
        rows_f, rows_b = pl.ds(r_f, span), pl.ds(r_b, span)
        qf, ff, vf = qf_ref[rows_f, :], ff_ref[rows_f, :], vf_ref[rows_f, :]
        qb, fb, vb = qb_ref[rows_b, :], fb_ref[rows_b, :], vb_ref[rows_b, :]
        chains = []
        for u in range(U):
            cf = slice(u * C, (u + 1) * C)
            chains.append((qf[cf], ff[cf], vf[cf], lbf, tri_f, False))
            cb = slice((U - 1 - u) * C, (U - u) * C)
            chains.append((qb[cb], fb[cb], vb[cb], lbb, tri_b, True))
        gates = [_hgrn_gates(q, f, lb, tri, rev) for q, f, _, lb, tri, rev in chains]
        mids = [_hgrn_scores(q, k, ch[2], b, ch[5]) for (q, k, b), ch in zip(gates, chains)]
        st = {False: sf_ref[...], True: sb_ref[...]}
        outs = {False: [], True: []}
        for (q_inter, decay, st_inc, scores), ch in zip(mids, chains):
            rev = ch[5]
            outs[rev].append(_hgrn_outputs(scores, ch[2], q_inter, st[rev], rev))
            st[rev] = st[rev] * decay + st_inc
        of_ref[rows_f, :] = jnp.concatenate(outs[False], axis=0)
        ob_ref[rows_b, :] = jnp.concatenate(outs[True][::-1], axis=0)
        sf_ref[...] = st[False]
        sb_ref[...] = st[True]
        return carry

    lax.fori_loop(0, n_spans, body, 0)


def _hgrn_recurrence(proj, lb, *, T=2048):
    B, S, D5 = proj.shape
    D = D5 // 5
    H = D // HG_HEAD_DIM
    T = min(T, S)
    nT = S // T
    blk = (None, T, HG_HEAD_DIM)
    fwd = lambda off: pl.BlockSpec(blk, lambda b, h, t: (b, t, off + h))
    bwd = lambda off: pl.BlockSpec(blk, lambda b, h, t: (b, nT - 1 - t, off + h))
    out_shape = jax.ShapeDtypeStruct((B, S, D), F32)
    return pl.pallas_call(
        functools.partial(_hgrn_kernel, T=T),
        grid=(B, H, nT),
        in_specs=[fwd(0), fwd(H), fwd(3 * H), bwd(0), bwd(2 * H), bwd(3 * H),
                  pl.BlockSpec((2, HG_HEAD_DIM), lambda b, h, t: (0, h))],
        out_specs=[pl.BlockSpec(blk, lambda b, h, t: (b, t, h)),
                   pl.BlockSpec(blk, lambda b, h, t: (b, nT - 1 - t, h))],
        out_shape=[out_shape, out_shape],
        scratch_shapes=[pltpu.VMEM((HG_HEAD_DIM, HG_HEAD_DIM), F32),
                        pltpu.VMEM((HG_HEAD_DIM, HG_HEAD_DIM), F32)],
        compiler_params=_params(("parallel", "parallel", "arbitrary")),
        name="hgrn_recurrence",
    )(proj, proj, proj, proj, proj, proj, lb)


def _hgrn_out_kernel(of_ref, ob_ref, gate_ref, x_ref, wn_ref, wo_ref, out_ref):
    o = of_ref[...] + ob_ref[...]
    D = o.shape[-1]
    parts = []
    for h in range(D // HG_HEAD_DIM):
        oh = o[:, h * HG_HEAD_DIM:(h + 1) * HG_HEAD_DIM]
        parts.append(oh * lax.rsqrt(jnp.mean(oh * oh, axis=-1, keepdims=True) + EPS))
    gate = gate_ref[...].astype(F32)
    y = (jnp.concatenate(parts, axis=-1) * wn_ref[...]) * (gate * _sigmoid(gate))
    out_ref[...] = x_ref[...] + _dot(y.astype(BF16), wo_ref[...])


def _hgrn_out(o_f, o_b, proj, x, wn_tiled, w_out, *, tm=512):
    B, S, D = x.shape
    tm = min(tm, S)
    row = pl.BlockSpec((None, tm, D), lambda b, i: (b, i, 0))
    return pl.pallas_call(
        _hgrn_out_kernel,
        grid=(B, S // tm),
        in_specs=[row, row,
                  pl.BlockSpec((None, tm, D), lambda b, i: (b, i, 4)),
                  row,
                  pl.BlockSpec((1, D), lambda b, i: (0, 0)),
                  pl.BlockSpec((D, D), lambda b, i: (0, 0))],
        out_specs=row,
        out_shape=jax.ShapeDtypeStruct((B, S, D), F32),
        compiler_params=_params(("parallel", "parallel")),
        name="hgrn_out",
    )(o_f, o_b, proj, x, wn_tiled, w_out)


def _swiglu_step(h_ref, wg_ref, wu_ref, wd_ref, acc_ref):
    h = h_ref[...]
    g = _dot(h, wg_ref[...])
    u = _dot(h, wu_ref[...])
    a = (g * _sigmoid(g)) * u
    acc_ref[...] += _dot(a.astype(BF16), wd_ref[...])


def _ffn_kernel(x_ref, nw_ref, wg_ref, wu_ref, wd_ref, o_ref, h_ref, acc_ref):
    j = pl.program_id(2)

    @pl.when(j == 0)
    def _():
        h_ref[...] = _rms(x_ref[...], nw_ref[...]).astype(BF16)
        acc_ref[...] = jnp.zeros_like(acc_ref)

    _swiglu_step(h_ref, wg_ref, wu_ref, wd_ref, acc_ref)

    @pl.when(j == pl.num_programs(2) - 1)
    def _():
        o_ref[...] = x_ref[...] + acc_ref[...]


def _ffn(x, nw, w_gate, w_up, w_down, *, tm=1024, tf=512):
    B, S, D = x.shape
    F = w_gate.shape[-1]
    tm = min(tm, S)
    row = pl.BlockSpec((None, tm, D), lambda b, i, j: (b, i, 0))
    return pl.pallas_call(
        _ffn_kernel,
        grid=(B, S // tm, F // tf),
        in_specs=[row,
                  pl.BlockSpec((1, D), lambda b, i, j: (0, 0)),
                  pl.BlockSpec((D, tf), lambda b, i, j: (0, j)),
                  pl.BlockSpec((D, tf), lambda b, i, j: (0, j)),
                  pl.BlockSpec((tf, D), lambda b, i, j: (j, 0))],
        out_specs=row,
        out_shape=jax.ShapeDtypeStruct((B, S, D), F32),
        scratch_shapes=[pltpu.VMEM((tm, D), BF16), pltpu.VMEM((tm, D), F32)],
        compiler_params=_params(("parallel", "parallel", "arbitrary")),
        name="swiglu_ffn",
    )(x, nw, w_gate, w_up, w_down)


MOE_RB = 512
MOE_TF = 896
ROUTER_TM = 512
DISPATCH_TM = 512
COMBINE_TM = 256


def _router_kernel(x_ref, nw_ref, wr_ref, tri_ref, w_ref, i_ref, cnt_ref, run_ref):
    @pl.when(pl.program_id(0) == 0)
    def _():
        run_ref[...] = jnp.zeros_like(run_ref)

    h = _rms(x_ref[...], nw_ref[...])
    logits = [jnp.sum(h * wr_ref[e:e + 1, :], axis=-1, keepdims=True) for e in range(N_EXPERTS)]

    def top(vals):
        m = functools.reduce(jnp.maximum, vals)
        idx = functools.reduce(
            jnp.minimum, [jnp.where(v == m, e, N_EXPERTS) for e, v in enumerate(vals)])
        return m, idx

    m1, i1 = top(logits)
    m2, i2 = top([jnp.where(i1 == e, -jnp.inf, v) for e, v in enumerate(logits)])
    e2 = jnp.exp(m2 - m1)
    w1 = 1.0 / (1.0 + e2)
    w2 = e2 / (1.0 + e2)

    tm = h.shape[0]
    lane = lax.broadcasted_iota(jnp.int32, (tm, LANES), 1)
    oh1 = jnp.where(lane == i1, 1.0, 0.0)
    oh2 = jnp.where(lane == i2, 1.0, 0.0)
    oh = oh1 + oh2
    before = _dot(tri_ref[...], oh.astype(BF16)) + run_ref[...]
    r1 = jnp.sum(oh1 * before, axis=-1, keepdims=True).astype(jnp.int32)
    r2 = jnp.sum(oh2 * before, axis=-1, keepdims=True).astype(jnp.int32)
    run_ref[...] += jnp.sum(oh, axis=0, keepdims=True)
    cnt_ref[...] = run_ref[...].astype(jnp.int32)
    w_ref[...] = jnp.where(lane == 0, w1, jnp.where(lane == 1, w2, 0.0))
    i_ref[...] = jnp.where(lane == 0, i1, jnp.where(lane == 1, i2,
                           jnp.where(lane == 2, r1, jnp.where(lane == 3, r2, 0))))


def _router(x2, nw, w_router_t, tri, *, tm=ROUTER_TM):
    N, D = x2.shape
    row = lambda w: pl.BlockSpec((tm, w), lambda i: (i, 0))
    return pl.pallas_call(
        _router_kernel,
        grid=(N // tm,),
        in_specs=[row(D),
                  pl.BlockSpec((1, D), lambda i: (0, 0)),
                  pl.BlockSpec((N_EXPERTS, D), lambda i: (0, 0)),
                  pl.BlockSpec((tm, tm), lambda i: (0, 0))],
        out_specs=[row(LANES), row(LANES), pl.BlockSpec((1, LANES), lambda i: (0, 0))],
        out_shape=[jax.ShapeDtypeStruct((N, LANES), F32),
                   jax.ShapeDtypeStruct((N, LANES), jnp.int32),
                   jax.ShapeDtypeStruct((1, LANES), jnp.int32)],
        scratch_shapes=[pltpu.VMEM((1, LANES), F32)],
        compiler_params=_params(("arbitrary",)),
        name="moe_router",
    )(x2, nw, w_router_t, tri)


def _row_copy(src_hbm, src_row, dst_ref, dst_row, sem):
    return pltpu.make_async_copy(src_hbm.at[pl.ds(src_row, 1)], dst_ref.at[pl.ds(dst_row, 1)], sem)


def _dispatch_kernel(dst_ref, x_hbm, xs_init_hbm, xs_hbm, sem, *, tm):
    del xs_init_hbm
    base = pl.program_id(0) * tm

    def issue(t, carry):
        _row_copy(x_hbm, base + t, xs_hbm, dst_ref[0, 2 * t], sem).start()
        _row_copy(x_hbm, base + t, xs_hbm, dst_ref[0, 2 * t + 1], sem).start()
        return carry

    lax.fori_loop(0, tm, issue, 0, unroll=8)
    pltpu.make_async_copy(x_hbm.at[pl.ds(0, 2 * tm)], xs_hbm.at[pl.ds(0, 2 * tm)], sem).wait()


def _dispatch(x2, dst, n_rows, *, tm=DISPATCH_TM):
    N, D = x2.shape
    return pl.pallas_call(
        functools.partial(_dispatch_kernel, tm=tm),
        grid=(N // tm,),
        in_specs=[pl.BlockSpec((None, 1, 2 * tm), lambda i: (i, 0, 0), memory_space=pltpu.SMEM),
                  pl.BlockSpec(memory_space=pl.ANY),
                  pl.BlockSpec(memory_space=pl.ANY)],
        out_specs=pl.BlockSpec(memory_space=pl.ANY),
        out_shape=jax.ShapeDtypeStruct((n_rows, D), F32),
        scratch_shapes=[pltpu.SemaphoreType.DMA(())],
        input_output_aliases={2: 0},
        compiler_params=_params(("arbitrary",)),
        name="moe_dispatch",
    )(dst.reshape(N // tm, 1, 2 * tm), x2, jnp.zeros((n_rows, D), F32))


def _expert_kernel(be_ref, nb_ref, x_ref, nw_ref, wg_ref, wu_ref, wd_ref, o_ref, h_ref, acc_ref):
    del be_ref
    j = pl.program_id(1)
    used = pl.program_id(0) < nb_ref[0]

    @pl.when(jnp.logical_and(jnp.logical_not(used), j == 0))
    def _():
        o_ref[...] = jnp.zeros_like(o_ref)

    @pl.when(used)
    def _():
        @pl.when(j == 0)
        def _():
            h_ref[...] = _rms(x_ref[...], nw_ref[...]).astype(BF16)
            acc_ref[...] = jnp.zeros_like(acc_ref)

        _swiglu_step(h_ref, wg_ref, wu_ref, wd_ref, acc_ref)

        @pl.when(j == pl.num_programs(1) - 1)
        def _():
            o_ref[...] = acc_ref[...]


def _experts(xs, nw, blk_expert, n_blk, w_gate, w_up, w_down, *, rb=MOE_RB, tf=MOE_TF):
    R, D = xs.shape
    F = w_gate.shape[-1]
    nj = F // tf
    used = lambda r, nb: jnp.minimum(r, nb[0] - 1)
    jj = lambda r, j, nb: jnp.where(r < nb[0], j, nj - 1)
    row = pl.BlockSpec((rb, D), lambda r, j, be, nb: (used(r, nb), 0))
    w_in = pl.BlockSpec((None, D, tf), lambda r, j, be, nb: (be[used(r, nb)], 0, jj(r, j, nb)))
    w_out = pl.BlockSpec((None, tf, D), lambda r, j, be, nb: (be[used(r, nb)], jj(r, j, nb), 0))
    return pl.pallas_call(
        _expert_kernel,
        grid_spec=pltpu.PrefetchScalarGridSpec(
            num_scalar_prefetch=2,
            grid=(R // rb, nj),
            in_specs=[row, pl.BlockSpec((1, D), lambda r, j, be, nb: (0, 0)), w_in, w_in, w_out],
            out_specs=pl.BlockSpec((rb, D), lambda r, j, be, nb: (r, 0)),
            scratch_shapes=[pltpu.VMEM((rb, D), BF16), pltpu.VMEM((rb, D), F32)]),
        out_shape=jax.ShapeDtypeStruct((R, D), F32),
        compiler_params=_params(("arbitrary", "arbitrary")),
        name="moe_experts",
    )(blk_expert, n_blk, xs, nw, w_gate, w_up, w_down)


def _combine_kernel(*refs, tm, final_norm):
    if final_norm:
        cur_ref, nxt_ref, w_ref, x_ref, y_hbm, fnw_ref, o_ref, ybuf, sem = refs
    else:
        cur_ref, nxt_ref, w_ref, x_ref, y_hbm, o_ref, ybuf, sem = refs
    i = pl.program_id(0)
    slot = i % 2

    def issue(idx_ref, s):
        def body(t, carry):
            _row_copy(y_hbm, idx_ref[0, 2 * t], ybuf.at[s], t, sem.at[s]).start()
            _row_copy(y_hbm, idx_ref[0, 2 * t + 1], ybuf.at[s], tm + t, sem.at[s]).start()
            return carry
        lax.fori_loop(0, tm, body, 0, unroll=8)

    @pl.when(i == 0)
    def _():
        issue(cur_ref, 0)

    @pl.when(i + 1 < pl.num_programs(0))
    def _():
        issue(nxt_ref, 1 - slot)

    pltpu.make_async_copy(y_hbm.at[pl.ds(0, 2 * tm)], ybuf.at[slot], sem.at[slot]).wait()
    w = w_ref[...]
    y = ybuf[slot]
    out = x_ref[...] + w[:, 0:1] * y[0:tm] + w[:, 1:2] * y[tm:2 * tm]
    if final_norm:
        out = _rms(out, fnw_ref[...])
    o_ref[...] = out


def _combine(x2, y, dst, wts, final_nw, *, tm=COMBINE_TM):
    N, D = x2.shape
    nt = N // tm
    idx = dst.reshape(nt, 1, 2 * tm)
    smem = lambda f: pl.BlockSpec((None, 1, 2 * tm), f, memory_space=pltpu.SMEM)
    in_specs = [smem(lambda i: (i, 0, 0)),
                smem(lambda i: (jnp.minimum(i + 1, nt - 1), 0, 0)),
                pl.BlockSpec((tm, LANES), lambda i: (i, 0)),
                pl.BlockSpec((tm, D), lambda i: (i, 0)),
                pl.BlockSpec(memory_space=pl.ANY)]
    args = [idx, idx, wts, x2, y]
    if final_nw is not None:
        in_specs.append(pl.BlockSpec((1, D), lambda i: (0, 0)))
        args.append(final_nw)
    return pl.pallas_call(
        functools.partial(_combine_kernel, tm=tm, final_norm=final_nw is not None),
        grid=(nt,),
        in_specs=in_specs,
        out_specs=pl.BlockSpec((tm, D), lambda i: (i, 0)),
        out_shape=jax.ShapeDtypeStruct((N, D), F32),
        scratch_shapes=[pltpu.VMEM((2, 2 * tm, D), F32), pltpu.SemaphoreType.DMA((2,))],
        compiler_params=_params(("arbitrary",)),
        name="moe_combine",
    )(*args)


def _moe(x, nw, w_router_t, tri, w_gate, w_up, w_down, final_nw):
    B, S, D = x.shape
    N = B * S
    rb = MOE_RB
    x2 = x.reshape(N, D)
    wts, meta, cnt = _router(x2, nw, w_router_t, tri)
    counts = cnt[0, :N_EXPERTS]
    group = ((counts + rb - 1) // rb) * rb
    ends = jnp.cumsum(group)
    offs = ends - group
    dst = jnp.stack([offs[meta[:, 0]] + meta[:, 2], offs[meta[:, 1]] + meta[:, 3]], axis=-1)
    n_blocks = (2 * N) // rb + N_EXPERTS
    blk_expert = jnp.minimum(
        jnp.searchsorted(ends // rb, jnp.arange(n_blocks, dtype=jnp.int32), side="right"),
        N_EXPERTS - 1).astype(jnp.int32)
    n_blk = (ends[-1:] // rb).astype(jnp.int32)
    xs = _dispatch(x2, dst, n_blocks * rb)
    y = _experts(xs, nw, blk_expert, n_blk, w_gate, w_up, w_down)
    return _combine(x2, y, dst, wts, final_nw).reshape(B, S, D)


def _attn_kernel(q_ref, k_ref, v_ref, kp_ref, kn_ref, vp_ref, vn_ref, o_ref, lse_ref,
                 kw_ref, vw_ref, *, dil, L, TQ, slopes):
    R, QB, KW = ATT_RADIUS, ATT_QB, ATT_KW
    n = pl.program_id(2)
    kw_ref[0:R, :] = kp_ref[...]
    kw_ref[R:R + TQ, :] = k_ref[...]
    kw_ref[R + TQ:, :] = kn_ref[...]
    vw_ref[0:R, :] = vp_ref[...]
    vw_ref[R:R + TQ, :] = v_ref[...]
    vw_ref[R + TQ:, :] = vn_ref[...]

    qrow = lax.broadcasted_iota(jnp.int32, (QB, KW), 0)
    kcol = lax.broadcasted_iota(jnp.int32, (QB, KW), 1)
    arel = jnp.abs(kcol - R - qrow)
    band = arel <= R
    nbias = (-float(dil)) * arel.astype(F32)
    lane = lax.broadcasted_iota(jnp.int32, (QB, LANES), 1)
    lo_half = lane < ATT_HEAD_DIM

    def sub(i, carry):
        r0 = pl.multiple_of(i * QB, QB)
        kpos = n * TQ + i * QB - R + kcol
        valid = band & (kpos >= 0) & (kpos < L)
        lse_tile = jnp.zeros((QB, LANES), F32)
        for p in range(ATT_HEADS // 2):
            cs = slice(p * LANES, (p + 1) * LANES)
            q2 = q_ref[pl.ds(r0, QB), cs] * 0.125
            k2 = kw_ref[pl.ds(r0, KW), cs]
            v2 = vw_ref[pl.ds(r0, KW), cs]
            outs = []
            for hh in range(2):
                h = 2 * p + hh
                qm = jnp.where(lo_half if hh == 0 else jnp.logical_not(lo_half), q2, 0.0)
                s = _dot_nt(qm.astype(BF16), k2) + slopes[h] * nbias
                s = jnp.where(valid, s, -jnp.inf)
                m = jnp.max(s, axis=-1, keepdims=True)
                pe = jnp.exp(s - m)
                den = jnp.sum(pe, axis=-1, keepdims=True)
                outs.append(_dot(pe.astype(BF16), v2) / den)
                lse_tile = jnp.where(lane == h, m + jnp.log(den), lse_tile)
            o_ref[pl.ds(r0, QB), cs] = jnp.where(lo_half, outs[0], outs[1])
        lse_ref[pl.ds(r0, QB), :] = lse_tile
        return carry

    lax.fori_loop(0, TQ // QB, sub, 0)


def _attn_group(qkv, dil, slopes, *, TQ=512):
    B, _, L, D3 = qkv.shape
    D = D3 // 3
    R = ATT_RADIUS
    TQ = min(TQ, L)
    nq = L // TQ
    hb = TQ // R
    last_hb = L // R - 1
    main = lambda c: pl.BlockSpec((None, None, TQ, D), lambda b, r, n: (b, r, n, c))
    prev = lambda c: pl.BlockSpec((None, None, R, D),
                                  lambda b, r, n: (b, r, jnp.maximum(n * hb - 1, 0), c))
    nxt = lambda c: pl.BlockSpec((None, None, R, D),
                                 lambda b, r, n: (b, r, jnp.minimum((n + 1) * hb, last_hb), c))
    kern = functools.partial(_attn_kernel, dil=dil, L=L, TQ=TQ, slopes=slopes)
    return pl.pallas_call(
        kern,
        grid=(B, dil, nq),
        in_specs=[main(0), main(1), main(2), prev(1), nxt(1), prev(2), nxt(2)],
        out_specs=[pl.BlockSpec((None, None, TQ, D), lambda b, r, n: (b, r, n, 0)),
                   pl.BlockSpec((None, None, TQ, LANES), lambda b, r, n: (b, r, n, 0))],
        out_shape=[jax.ShapeDtypeStruct((B, dil, L, D), F32),
                   jax.ShapeDtypeStruct((B, dil, L, LANES), F32)],
        scratch_shapes=[pltpu.VMEM((TQ + 2 * R, D), BF16), pltpu.VMEM((TQ + 2 * R, D), BF16)],
        compiler_params=_params(("parallel", "parallel", "parallel")),
        name="dilated_attn",
    )(qkv, qkv, qkv, qkv, qkv, qkv, qkv)


def _attn_merge_kernel(o0_ref, o1_ref, o2_ref, l0_ref, l1_ref, l2_ref, x_ref, ex_ref, wo_ref,
                       out_ref, s1_ref, s2_ref, t1_ref, t2_ref, *, tm):
    n_chunks = s1_ref.shape[0]

    def interleave(src_ref, lse_src_ref, dst_ref, lse_dst_ref, dil):
        rows = tm // dil
        for r in range(dil):
            lse_dst_ref[pl.ds(r, rows, stride=dil), :] = lse_src_ref[r]
            for c in range(n_chunks):
                dst_ref[c, pl.ds(r, rows, stride=dil), :] = src_ref[r, :, c * LANES:(c + 1) * LANES]
        return jnp.concatenate([dst_ref[c] for c in range(n_chunks)], axis=1)

    o1 = interleave(o1_ref, l1_ref, s1_ref, t1_ref, ATT_DILATIONS[1])
    o2 = interleave(o2_ref, l2_ref, s2_ref, t2_ref, ATT_DILATIONS[2])
    la, lb, lc = l0_ref[0], t1_ref[...], t2_ref[...]
    m = jnp.maximum(jnp.maximum(la, lb), lc)
    ea, eb, ec = jnp.exp(la - m), jnp.exp(lb - m), jnp.exp(lc - m)
    den = ea + eb + ec
    ex = ex_ref[...]

    def expand(w):
        hi = w.astype(BF16)
        lo = (w - hi.astype(F32)).astype(BF16)
        return _dot(hi, ex) + _dot(lo, ex)

    o = expand(ea / den) * o0_ref[0] + expand(eb / den) * o1 + expand(ec / den) * o2
    out_ref[...] = x_ref[...] + _dot(o.astype(BF16), wo_ref[...])


def _attn_merge(outs, lses, x, expand_mat, w_out, *, tm=512):
    B, S, D = x.shape
    tm = min(tm, S)
    d0, d1, d2 = ATT_DILATIONS
    grp = lambda d, w: pl.BlockSpec((None, d, tm // d, w), lambda b, i: (b, 0, i, 0))
    row = pl.BlockSpec((None, tm, D), lambda b, i: (b, i, 0))
    return pl.pallas_call(
        functools.partial(_attn_merge_kernel, tm=tm),
        grid=(B, S // tm),
        in_specs=[grp(d0, D), grp(d1, D), grp(d2, D), grp(d0, LANES), grp(d1, LANES),
                  grp(d2, LANES), row,
                  pl.BlockSpec((LANES, D), lambda b, i: (0, 0)),
                  pl.BlockSpec((D, D), lambda b, i: (0, 0))],
        out_specs=row,
        out_shape=jax.ShapeDtypeStruct((B, S, D), F32),
        scratch_shapes=[pltpu.VMEM((D // LANES, tm, LANES), F32),
                        pltpu.VMEM((D // LANES, tm, LANES), F32),
                        pltpu.VMEM((tm, LANES), F32), pltpu.VMEM((tm, LANES), F32)],
        compiler_params=_params(("parallel", "parallel")),
        name="attn_merge",
    )(*outs, *lses, x, expand_mat, w_out)


def _alibi_slopes(n):
    return tuple(float(s) for s in np.asarray(2.0 ** (-8.0 * (np.arange(n) + 1) / n), np.float32))


def _trunk(x, p):
    depth = p["norm_mix"].shape[0]
    D = x.shape[-1]
    slopes = _alibi_slopes(ATT_HEADS)
    for i in range(depth):
        a = i // 2
        nw_mix = p["norm_mix"][i][None, :]
        nw_ffn = p["norm_ffn"][i][None, :]
        if i % 2 == 0:
            proj = _norm_proj(x, nw_mix, p["hg_w_in"][a])[:, 0]
            o_f, o_b = _hgrn_recurrence(proj, p["lbs"][i])
            x = _hgrn_out(o_f, o_b, proj, x, p["hg_norm_tiled"][a], p["hg_w_out"][a])
            x = _ffn(x, nw_ffn, p["ffn_w_gate"][a], p["ffn_w_up"][a], p["ffn_w_down"][a])
        else:
            outs, lses = [], []
            for g, dil in enumerate(ATT_DILATIONS):
                w_g = p["attn_w_in"][a][:, 3 * D * g:3 * D * (g + 1)]
                qkv = _norm_proj(x, nw_mix, w_g, dil=dil)
                o_g, lse_g = _attn_group(qkv, dil, slopes)
                outs.append(o_g)
                lses.append(lse_g)
            x = _attn_merge(outs, lses, x, p["expand_mat"], p["attn_w_out"][a])
            final_nw = p["norm_out"][None, :] if i == depth - 1 else None
            x = _moe(x, nw_ffn, p["moe_router_t"][a], p["rank_tri"], p["moe_w_gate"][a],
                     p["moe_w_up"][a], p["moe_w_down"][a], final_nw)
    return x


def kernel(x_prompt, x_sample, norm_mix, norm_ffn, norm_out, hg_w_in, hg_lower_bounds, hg_norm,
           hg_w_out, attn_w_in, attn_w_out, ffn_w_gate, ffn_w_up, ffn_w_down, moe_router,
           moe_w_gate, moe_w_up, moe_w_down):
    D = x_prompt.shape[-1]
    assert norm_mix.shape[0] % 2 == 0, "trunk ends on an attention/MoE layer"
    pr = jax.nn.softmax(hg_lower_bounds.astype(F32), axis=0)
    head_of_lane = np.arange(D) // ATT_HEAD_DIM
    expand_mat = jnp.asarray(np.arange(LANES)[:, None] == head_of_lane[None, :], BF16)
    p = dict(
        norm_mix=norm_mix, norm_ffn=norm_ffn, norm_out=norm_out,
        lbs=jnp.cumsum(pr, axis=0) - pr[0],
        hg_w_in=hg_w_in.astype(BF16), hg_w_out=hg_w_out.astype(BF16),
        hg_norm_tiled=jnp.tile(hg_norm.astype(F32), (1, D // HG_HEAD_DIM))[:, None, :],
        attn_w_in=attn_w_in.astype(BF16), attn_w_out=attn_w_out.astype(BF16),
        ffn_w_gate=ffn_w_gate.astype(BF16), ffn_w_up=ffn_w_up.astype(BF16),
        ffn_w_down=ffn_w_down.astype(BF16),
        moe_router_t=jnp.swapaxes(moe_router, 1, 2).astype(F32),
        moe_w_gate=moe_w_gate.astype(BF16), moe_w_up=moe_w_up.astype(BF16),
        moe_w_down=moe_w_down.astype(BF16),
        expand_mat=expand_mat,
        rank_tri=jnp.asarray(np.tri(ROUTER_TM, k=-1), BF16),
    )
    return (_trunk(x_prompt, p), _trunk(x_sample, p))
```

```python
import functools

import jax
import jax.numpy as jnp
import numpy as np
from jax import lax
from jax.experimental import pallas as pl
from jax.experimental.pallas import tpu as pltpu

F32 = jnp.float32
BF16 = jnp.bfloat16

EPS = 1e-6
HG_HEAD_DIM = 128
HG_CHUNK = 64
HG_SUB = 16
HG_EXP_CLAMP = 80.0
HG_UNROLL = 8
ATT_HEADS = 16
ATT_HEAD_DIM = 64
ATT_RADIUS = 64
ATT_DILATIONS = (1, 4, 16)
ATT_QB = 128
ATT_KW = ATT_QB + 2 * ATT_RADIUS
N_EXPERTS = 8
LANES = 128
VMEM_LIMIT = 56 * 1024 * 1024


def _params(sem):
    return pltpu.CompilerParams(dimension_semantics=sem, vmem_limit_bytes=VMEM_LIMIT)


def _rms(x, w):
    return (x * lax.rsqrt(jnp.mean(x * x, axis=-1, keepdims=True) + EPS)) * w


def _sigmoid(x):
    return 1.0 / (1.0 + jnp.exp(-x))


def _dot(a, b):
    return jnp.dot(a, b, preferred_element_type=F32)


def _dot_nt(a, b):
    return lax.dot_general(a, b, (((1,), (1,)), ((), ())), preferred_element_type=F32)


def _dot_tn(a, b):
    return lax.dot_general(a, b, (((0,), (0,)), ((), ())), preferred_element_type=F32)


def _norm_proj_kernel(x_ref, nw_ref, w_ref, o_ref, h_ref, *xs_ref, dil, tm):
    rows = tm // dil

    @pl.when(pl.program_id(2) == 0)
    def _():
        nw = nw_ref[...]
        if dil == 1:
            h_ref[...] = _rms(x_ref[...], nw).astype(BF16)
        else:
            xs = xs_ref[0]
            n_chunks = xs.shape[0]
            for c in range(n_chunks):
                xs[c] = x_ref[:, c * LANES:(c + 1) * LANES]
            for r in range(dil):
                xr = jnp.concatenate(
                    [xs[c, pl.ds(r, rows, stride=dil), :] for c in range(n_chunks)], axis=1)
                h_ref[r * rows:(r + 1) * rows, :] = _rms(xr, nw).astype(BF16)

    acc = _dot(h_ref[...], w_ref[...])
    if dil == 1:
        o_ref[0] = acc.astype(o_ref.dtype)
    else:
        for r in range(dil):
            o_ref[r] = acc[r * rows:(r + 1) * rows].astype(o_ref.dtype)


def _norm_proj(x, nw, w, *, dil=1, tm=1024, tn=1024, out_dtype=BF16):
    B, S, D = x.shape
    N = w.shape[1]
    tm = min(tm, S)
    tn = min(tn, N)
    kern = functools.partial(_norm_proj_kernel, dil=dil, tm=tm)
    return pl.pallas_call(
        kern,
        grid=(B, S // tm, N // tn),
        in_specs=[
            pl.BlockSpec((None, tm, D), lambda b, i, j: (b, i, 0)),
            pl.BlockSpec((1, D), lambda b, i, j: (0, 0)),
            pl.BlockSpec((D, tn), lambda b, i, j: (0, j)),
        ],
        out_specs=pl.BlockSpec((None, dil, tm // dil, tn), lambda b, i, j: (b, 0, i, j)),
        out_shape=jax.ShapeDtypeStruct((B, dil, S // dil, N), out_dtype),
        scratch_shapes=[pltpu.VMEM((tm, D), BF16)]
        + ([pltpu.VMEM((D // LANES, tm, LANES), F32)] if dil > 1 else []),
        compiler_params=_params(("parallel", "parallel", "arbitrary")),
        name="norm_proj",
    )(x, nw, w)


def _hgrn_gates(q, f_raw, lb, tri, rev):
    q = q.astype(F32)
    q = q * _sigmoid(q)
    f = lb + (1.0 - lb) * _sigmoid(f_raw.astype(F32))
    g = jnp.log(f)
    g1 = g.astype(BF16)
    d1 = g - g1.astype(F32)
    g2 = d1.astype(BF16)
    g3 = (d1 - g2.astype(F32)).astype(BF16)
    b = _dot(tri, g1) + _dot(tri, g2) + _dot(tri, g3)
    return q, 1.0 - f, b


def _hgrn_scores(q, k, v, b, rev):
    C, SUB = HG_CHUNK, HG_SUB
    b_tot = b[0:1, :] if rev else b[C - 1:C, :]
    q_inter = (q * jnp.exp(b)).astype(BF16)
    kk = (k * jnp.exp(b_tot - b)).astype(BF16)
    st_inc = _dot_tn(v, kk)
    scores = []
    for i in range(C // SUB):
        lo, hi = i * SUB, (i + 1) * SUB
        if rev:
            klo, khi, mid = lo, C, lo + SUB // 2
        else:
            klo, khi, mid = 0, hi, lo + SUB // 2 - 1
        mu = b[mid:mid + 1, :]
        qi = (q[lo:hi] * jnp.exp(jnp.minimum(b[lo:hi] - mu, HG_EXP_CLAMP))).astype(BF16)
        ki = (k[klo:khi] * jnp.exp(jnp.minimum(mu - b[klo:khi], HG_EXP_CLAMP))).astype(BF16)
        scores.append(_dot_nt(qi, ki))
    return q_inter, jnp.exp(b_tot), st_inc, scores


def _hgrn_outputs(scores, v, q_inter, st, rev):
    C, SUB = HG_CHUNK, HG_SUB
    outs = []
    for i, a in enumerate(scores):
        lo = i * SUB
        klo, khi = (lo, C) if rev else (0, lo + SUB)
        t_idx = lo + lax.broadcasted_iota(jnp.int32, a.shape, 0)
        s_idx = klo + lax.broadcasted_iota(jnp.int32, a.shape, 1)
        a = jnp.where(s_idx >= t_idx if rev else s_idx <= t_idx, a, 0.0)
        outs.append(_dot(a.astype(BF16), v[klo:khi]))
    return _dot_nt(q_inter, st.astype(BF16)) + jnp.concatenate(outs, axis=0)


def _hgrn_kernel(qf_ref, ff_ref, vf_ref, qb_ref, fb_ref, vb_ref, lb_ref, of_ref, ob_ref,
                 sf_ref, sb_ref, *, T):
    C, U = HG_CHUNK, HG_UNROLL
    span = C * U

    @pl.when(pl.program_id(2) == 0)
    def _():
        sf_ref[...] = jnp.zeros_like(sf_ref)
        sb_ref[...] = jnp.zeros_like(sb_ref)

    row = lax.broadcasted_iota(jnp.int32, (C, C), 0)
    col = lax.broadcasted_iota(jnp.int32, (C, C), 1)
    tri_f = jnp.where(col <= row, 1.0, 0.0).astype(BF16)
    tri_b = jnp.where(col >= row, 1.0, 0.0).astype(BF16)
    lbf = lb_ref[0:1, :]
    lbb = lb_ref[1:2, :]
    n_spans = T // span

    def body(it, carry):
        r_f = pl.multiple_of(it * span, span)
        r_b = pl.multiple_of((n_spans - 1 - it) * span, span)
        rows_f, rows_b = pl.ds(r_f, span), pl.ds(r_b, span)
        qf, ff, vf = qf_ref[rows_f, :], ff_ref[rows_f, :], vf_ref[rows_f, :]
        qb, fb, vb = qb_ref[rows_b, :], fb_ref[rows_b, :], vb_ref[rows_b, :]
        chains = []
        for u in range(U):
            cf = slice(u * C, (u + 1) * C)
            chains.append((qf[cf], ff[cf], vf[cf], lbf, tri_f, False))
            cb = slice((U - 1 - u) * C, (U - u) * C)
            chains.append((qb[cb], fb[cb], vb[cb], lbb, tri_b, True))
        gates = [_hgrn_gates(q, f, lb, tri, rev) for q, f, _, lb, tri, rev in chains]
        mids = [_hgrn_scores(q, k, ch[2], b, ch[5]) for (q, k, b), ch in zip(gates, chains)]
        st = {False: sf_ref[...], True: sb_ref[...]}
        outs = {False: [], True: []}
        for (q_inter, decay, st_inc, scores), ch in zip(mids, chains):
            rev = ch[5]
            outs[rev].append(_hgrn_outputs(scores, ch[2], q_inter, st[rev], rev))
            st[rev] = st[rev] * decay + st_inc
        of_ref[rows_f, :] = jnp.concatenate(outs[False], axis=0)
        ob_ref[rows_b, :] = jnp.concatenate(outs[True][::-1], axis=0)
        sf_ref[...] = st[False]
        sb_ref[...] = st[True]
        return carry

    lax.fori_loop(0, n_spans, body, 0)


def _hgrn_recurrence(proj, lb, *, T=2048):
    B, S, D5 = proj.shape
    D = D5 // 5
    H = D // HG_HEAD_DIM
    T = min(T, S)
    nT = S // T
    blk = (None, T, HG_HEAD_DIM)
    fwd = lambda off: pl.BlockSpec(blk, lambda b, h, t: (b, t, off + h))
    bwd = lambda off: pl.BlockSpec(blk, lambda b, h, t: (b, nT - 1 - t, off + h))
    out_shape = jax.ShapeDtypeStruct((B, S, D), F32)
    return pl.pallas_call(
        functools.partial(_hgrn_kernel, T=T),
        grid=(B, H, nT),
        in_specs=[fwd(0), fwd(H), fwd(3 * H), bwd(0), bwd(2 * H), bwd(3 * H),
                  pl.BlockSpec((2, HG_HEAD_DIM), lambda b, h, t: (0, h))],
        out_specs=[pl.BlockSpec(blk, lambda b, h, t: (b, t, h)),
                   pl.BlockSpec(blk, lambda b, h, t: (b, nT - 1 - t, h))],
        out_shape=[out_shape, out_shape],
        scratch_shapes=[pltpu.VMEM((HG_HEAD_DIM, HG_HEAD_DIM), F32),
                        pltpu.VMEM((HG_HEAD_DIM, HG_HEAD_DIM), F32)],
        compiler_params=_params(("parallel", "parallel", "arbitrary")),
        name="hgrn_recurrence",
    )(proj, proj, proj, proj, proj, proj, lb)


def _hgrn_out_kernel(of_ref, ob_ref, gate_ref, x_ref, wn_ref, wo_ref, out_ref):
    o = of_ref[...] + ob_ref[...]
    D = o.shape[-1]
    parts = []
    for h in range(D // HG_HEAD_DIM):
        oh = o[:, h * HG_HEAD_DIM:(h + 1) * HG_HEAD_DIM]
        parts.append(oh * lax.rsqrt(jnp.mean(oh * oh, axis=-1, keepdims=True) + EPS))
    gate = gate_ref[...].astype(F32)
    y = (jnp.concatenate(parts, axis=-1) * wn_ref[...]) * (gate * _sigmoid(gate))
    out_ref[...] = x_ref[...] + _dot(y.astype(BF16), wo_ref[...])


def _hgrn_out(o_f, o_b, proj, x, wn_tiled, w_out, *, tm=512):
    B, S, D = x.shape
    tm = min(tm, S)
    row = pl.BlockSpec((None, tm, D), lambda b, i: (b, i, 0))
    return pl.pallas_call(
        _hgrn_out_kernel,
        grid=(B, S // tm),
        in_specs=[row, row,
                  pl.BlockSpec((None, tm, D), lambda b, i: (b, i, 4)),
                  row,
                  pl.BlockSpec((1, D), lambda b, i: (0, 0)),
                  pl.BlockSpec((D, D), lambda b, i: (0, 0))],
        out_specs=row,
        out_shape=jax.ShapeDtypeStruct((B, S, D), F32),
        compiler_params=_params(("parallel", "parallel")),
        name="hgrn_out",
    )(o_f, o_b, proj, x, wn_tiled, w_out)


def _swiglu_step(h_ref, wg_ref, wu_ref, wd_ref, acc_ref):
    h = h_ref[...]
    g = _dot(h, wg_ref[...])
    u = _dot(h, wu_ref[...])
    a = (g * _sigmoid(g)) * u
    acc_ref[...] += _dot(a.astype(BF16), wd_ref[...])


def _ffn_kernel(x_ref, nw_ref, wg_ref, wu_ref, wd_ref, o_ref, h_ref, acc_ref):
    j = pl.program_id(2)

    @pl.when(j == 0)
    def _():
        h_ref[...] = _rms(x_ref[...], nw_ref[...]).astype(BF16)
        acc_ref[...] = jnp.zeros_like(acc_ref)

    _swiglu_step(h_ref, wg_ref, wu_ref, wd_ref, acc_ref)

    @pl.when(j == pl.num_programs(2) - 1)
    def _():
        o_ref[...] = x_ref[...] + acc_ref[...]


def _ffn(x, nw, w_gate, w_up, w_down, *, tm=1024, tf=896):
    B, S, D = x.shape
    F = w_gate.shape[-1]
    tm = min(tm, S)
    row = pl.BlockSpec((None, tm, D), lambda b, i, j: (b, i, 0))
    return pl.pallas_call(
        _ffn_kernel,
        grid=(B, S // tm, F // tf),
        in_specs=[row,
                  pl.BlockSpec((1, D), lambda b, i, j: (0, 0)),
                  pl.BlockSpec((D, tf), lambda b, i, j: (0, j)),
                  pl.BlockSpec((D, tf), lambda b, i, j: (0, j)),
                  pl.BlockSpec((tf, D), lambda b, i, j: (j, 0))],
        out_specs=row,
        out_shape=jax.ShapeDtypeStruct((B, S, D), F32),
        scratch_shapes=[pltpu.VMEM((tm, D), BF16), pltpu.VMEM((tm, D), F32)],
        compiler_params=_params(("parallel", "parallel", "arbitrary")),
        name="swiglu_ffn",
    )(x, nw, w_gate, w_up, w_down)


MOE_RB = 512
MOE_TF = 1792
ROUTER_TM = 512
DISPATCH_TM = 512
COMBINE_TM = 256


def _router_kernel(x_ref, nw_ref, wr_ref, tri_ref, w_ref, i_ref, cnt_ref, run_ref):
    @pl.when(pl.program_id(0) == 0)
    def _():
        run_ref[...] = jnp.zeros_like(run_ref)

    h = _rms(x_ref[...], nw_ref[...])
    logits = [jnp.sum(h * wr_ref[e:e + 1, :], axis=-1, keepdims=True) for e in range(N_EXPERTS)]

    def top(vals):
        m = functools.reduce(jnp.maximum, vals)
        idx = functools.reduce(
            jnp.minimum, [jnp.where(v == m, e, N_EXPERTS) for e, v in enumerate(vals)])
        return m, idx

    m1, i1 = top(logits)
    m2, i2 = top([jnp.where(i1 == e, -jnp.inf, v) for e, v in enumerate(logits)])
    e2 = jnp.exp(m2 - m1)
    w1 = 1.0 / (1.0 + e2)
    w2 = e2 / (1.0 + e2)

    tm = h.shape[0]
    lane = lax.broadcasted_iota(jnp.int32, (tm, LANES), 1)
    oh1 = jnp.where(lane == i1, 1.0, 0.0)
    oh2 = jnp.where(lane == i2, 1.0, 0.0)
    oh = oh1 + oh2
    before = _dot(tri_ref[...], oh.astype(BF16)) + run_ref[...]
    r1 = jnp.sum(oh1 * before, axis=-1, keepdims=True).astype(jnp.int32)
    r2 = jnp.sum(oh2 * before, axis=-1, keepdims=True).astype(jnp.int32)
    run_ref[...] += jnp.sum(oh, axis=0, keepdims=True)
    cnt_ref[...] = run_ref[...].astype(jnp.int32)
    w_ref[...] = jnp.where(lane == 0, w1, jnp.where(lane == 1, w2, 0.0))
    i_ref[...] = jnp.where(lane == 0, i1, jnp.where(lane == 1, i2,
                           jnp.where(lane == 2, r1, jnp.where(lane == 3, r2, 0))))


def _router(x2, nw, w_router_t, tri, *, tm=ROUTER_TM):
    N, D = x2.shape
    row = lambda w: pl.BlockSpec((tm, w), lambda i: (i, 0))
    return pl.pallas_call(
        _router_kernel,
        grid=(N // tm,),
        in_specs=[row(D),
                  pl.BlockSpec((1, D), lambda i: (0, 0)),
                  pl.BlockSpec((N_EXPERTS, D), lambda i: (0, 0)),
                  pl.BlockSpec((tm, tm), lambda i: (0, 0))],
        out_specs=[row(LANES), row(LANES), pl.BlockSpec((1, LANES), lambda i: (0, 0))],
        out_shape=[jax.ShapeDtypeStruct((N, LANES), F32),
                   jax.ShapeDtypeStruct((N, LANES), jnp.int32),
                   jax.ShapeDtypeStruct((1, LANES), jnp.int32)],
        scratch_shapes=[pltpu.VMEM((1, LANES), F32)],
        compiler_params=_params(("arbitrary",)),
        name="moe_router",
    )(x2, nw, w_router_t, tri)


def _row_copy(src_hbm, src_row, dst_ref, dst_row, sem):
    return pltpu.make_async_copy(src_hbm.at[pl.ds(src_row, 1)], dst_ref.at[pl.ds(dst_row, 1)], sem)


def _dispatch_kernel(dst_ref, x_ref, xs_init_hbm, xs_hbm, sem, *, tm):
    del xs_init_hbm

    def issue(t, carry):
        _row_copy(x_ref, t, xs_hbm, dst_ref[0, 2 * t], sem).start()
        _row_copy(x_ref, t, xs_hbm, dst_ref[0, 2 * t + 1], sem).start()
        return carry

    lax.fori_loop(0, tm, issue, 0, unroll=8)
    for _ in range(2):
        pltpu.make_async_copy(x_ref, xs_hbm.at[pl.ds(0, tm)], sem).wait()


def _dispatch(x2, dst, n_rows, *, tm=DISPATCH_TM):
    N, D = x2.shape
    return pl.pallas_call(
        functools.partial(_dispatch_kernel, tm=tm),
        grid=(N // tm,),
        in_specs=[pl.BlockSpec((None, 1, 2 * tm), lambda i: (i, 0, 0), memory_space=pltpu.SMEM),
                  pl.BlockSpec((tm, D), lambda i: (i, 0)),
                  pl.BlockSpec(memory_space=pl.ANY)],
        out_specs=pl.BlockSpec(memory_space=pl.ANY),
        out_shape=jax.ShapeDtypeStruct((n_rows, D), F32),
        scratch_shapes=[pltpu.SemaphoreType.DMA(())],
        input_output_aliases={2: 0},
        compiler_params=_params(("arbitrary",)),
        name="moe_dispatch",
    )(dst.reshape(N // tm, 1, 2 * tm), x2, jnp.zeros((n_rows, D), F32))


def _expert_kernel(be_ref, nb_ref, x_ref, nw_ref, wg_ref, wu_ref, wd_ref, o_ref, h_ref, acc_ref):
    del be_ref
    j = pl.program_id(1)
    used = pl.program_id(0) < nb_ref[0]

    @pl.when(jnp.logical_and(jnp.logical_not(used), j == 0))
    def _():
        o_ref[...] = jnp.zeros_like(o_ref)

    @pl.when(used)
    def _():
        @pl.when(j == 0)
        def _():
            h_ref[...] = _rms(x_ref[...], nw_ref[...]).astype(BF16)
            acc_ref[...] = jnp.zeros_like(acc_ref)

        _swiglu_step(h_ref, wg_ref, wu_ref, wd_ref, acc_ref)

        @pl.when(j == pl.num_programs(1) - 1)
        def _():
            o_ref[...] = acc_ref[...]


def _experts(xs, nw, blk_expert, n_blk, w_gate, w_up, w_down, *, rb=MOE_RB, tf=MOE_TF):
    R, D = xs.shape
    F = w_gate.shape[-1]
    nj = F // tf
    used = lambda r, nb: jnp.minimum(r, nb[0] - 1)
    jj = lambda r, j, nb: jnp.where(r < nb[0], j, nj - 1)
    row = pl.BlockSpec((rb, D), lambda r, j, be, nb: (used(r, nb), 0))
    w_in = pl.BlockSpec((None, D, tf), lambda r, j, be, nb: (be[used(r, nb)], 0, jj(r, j, nb)))
    w_out = pl.BlockSpec((None, tf, D), lambda r, j, be, nb: (be[used(r, nb)], jj(r, j, nb), 0))
    return pl.pallas_call(
        _expert_kernel,
        grid_spec=pltpu.PrefetchScalarGridSpec(
            num_scalar_prefetch=2,
            grid=(R // rb, nj),
            in_specs=[row, pl.BlockSpec((1, D), lambda r, j, be, nb: (0, 0)), w_in, w_in, w_out],
            out_specs=pl.BlockSpec((rb, D), lambda r, j, be, nb: (r, 0)),
            scratch_shapes=[pltpu.VMEM((rb, D), BF16), pltpu.VMEM((rb, D), F32)]),
        out_shape=jax.ShapeDtypeStruct((R, D), F32),
        compiler_params=_params(("arbitrary", "arbitrary")),
        name="moe_experts",
    )(blk_expert, n_blk, xs, nw, w_gate, w_up, w_down)


def _combine_kernel(*refs, tm, final_norm):
    if final_norm:
        cur_ref, nxt_ref, w_ref, x_ref, y_hbm, fnw_ref, o_ref, ybuf, sem = refs
    else:
        cur_ref, nxt_ref, w_ref, x_ref, y_hbm, o_ref, ybuf, sem = refs
    i = pl.program_id(0)
    slot = i % 2

    def issue(idx_ref, s):
        def body(t, carry):
            _row_copy(y_hbm, idx_ref[0, 2 * t], ybuf.at[s], t, sem.at[s]).start()
            _row_copy(y_hbm, idx_ref[0, 2 * t + 1], ybuf.at[s], tm + t, sem.at[s]).start()
            return carry
        lax.fori_loop(0, tm, body, 0, unroll=8)

    @pl.when(i == 0)
    def _():
        issue(cur_ref, 0)

    @pl.when(i + 1 < pl.num_programs(0))
    def _():
        issue(nxt_ref, 1 - slot)

    pltpu.make_async_copy(y_hbm.at[pl.ds(0, 2 * tm)], ybuf.at[slot], sem.at[slot]).wait()
    w = w_ref[...]
    y = ybuf[slot]
    out = x_ref[...] + w[:, 0:1] * y[0:tm] + w[:, 1:2] * y[tm:2 * tm]
    if final_norm:
        out = _rms(out, fnw_ref[...])
    o_ref[...] = out


def _combine(x2, y, dst, wts, final_nw, *, tm=COMBINE_TM):
    N, D = x2.shape
    nt = N // tm
    idx = dst.reshape(nt, 1, 2 * tm)
    smem = lambda f: pl.BlockSpec((None, 1, 2 * tm), f, memory_space=pltpu.SMEM)
    in_specs = [smem(lambda i: (i, 0, 0)),
                smem(lambda i: (jnp.minimum(i + 1, nt - 1), 0, 0)),
                pl.BlockSpec((tm, LANES), lambda i: (i, 0)),
                pl.BlockSpec((tm, D), lambda i: (i, 0)),
                pl.BlockSpec(memory_space=pl.ANY)]
    args = [idx, idx, wts, x2, y]
    if final_nw is not None:
        in_specs.append(pl.BlockSpec((1, D), lambda i: (0, 0)))
        args.append(final_nw)
    return pl.pallas_call(
        functools.partial(_combine_kernel, tm=tm, final_norm=final_nw is not None),
        grid=(nt,),
        in_specs=in_specs,
        out_specs=pl.BlockSpec((tm, D), lambda i: (i, 0)),
        out_shape=jax.ShapeDtypeStruct((N, D), F32),
        scratch_shapes=[pltpu.VMEM((2, 2 * tm, D), F32), pltpu.SemaphoreType.DMA((2,))],
        compiler_params=_params(("arbitrary",)),
        name="moe_combine",
    )(*args)


def _moe(x, nw, w_router_t, tri, w_gate, w_up, w_down, final_nw):
    B, S, D = x.shape
    N = B * S
    rb = MOE_RB
    x2 = x.reshape(N, D)
    wts, meta, cnt = _router(x2, nw, w_router_t, tri)
    counts = cnt[0, :N_EXPERTS]
    group = ((counts + rb - 1) // rb) * rb
    ends = jnp.cumsum(group)
    offs = ends - group
    dst = jnp.stack([offs[meta[:, 0]] + meta[:, 2], offs[meta[:, 1]] + meta[:, 3]], axis=-1)
    n_blocks = (2 * N) // rb + N_EXPERTS
    blk_expert = jnp.minimum(
        jnp.searchsorted(ends // rb, jnp.arange(n_blocks, dtype=jnp.int32), side="right"),
        N_EXPERTS - 1).astype(jnp.int32)
    n_blk = (ends[-1:] // rb).astype(jnp.int32)
    xs = _dispatch(x2, dst, n_blocks * rb)
    y = _experts(xs, nw, blk_expert, n_blk, w_gate, w_up, w_down)
    return _combine(x2, y, dst, wts, final_nw).reshape(B, S, D)


def _attn_kernel(q_ref, k_ref, v_ref, kp_ref, kn_ref, vp_ref, vn_ref, o_ref, lse_ref,
                 kw_ref, vw_ref, *, dil, L, TQ, slopes):
    R, QB, KW = ATT_RADIUS, ATT_QB, ATT_KW
    n = pl.program_id(2)
    kw_ref[0:R, :] = kp_ref[...]
    kw_ref[R:R + TQ, :] = k_ref[...]
    kw_ref[R + TQ:, :] = kn_ref[...]
    vw_ref[0:R, :] = vp_ref[...]
    vw_ref[R:R + TQ, :] = v_ref[...]
    vw_ref[R + TQ:, :] = vn_ref[...]

    qrow = lax.broadcasted_iota(jnp.int32, (QB, KW), 0)
    kcol = lax.broadcasted_iota(jnp.int32, (QB, KW), 1)
    arel = jnp.abs(kcol - R - qrow)
    band = arel <= R
    nbias = (-float(dil)) * arel.astype(F32)
    lane = lax.broadcasted_iota(jnp.int32, (QB, LANES), 1)
    lo_half = lane < ATT_HEAD_DIM

    def sub(i, carry):
        r0 = pl.multiple_of(i * QB, QB)
        kpos = n * TQ + i * QB - R + kcol
        valid = band & (kpos >= 0) & (kpos < L)
        lse_tile = jnp.zeros((QB, LANES), F32)
        for p in range(ATT_HEADS // 2):
            cs = slice(p * LANES, (p + 1) * LANES)
            q2 = q_ref[pl.ds(r0, QB), cs] * 0.125
            k2 = kw_ref[pl.ds(r0, KW), cs]
            v2 = vw_ref[pl.ds(r0, KW), cs]
            outs = []
            for hh in range(2):
                h = 2 * p + hh
                qm = jnp.where(lo_half if hh == 0 else jnp.logical_not(lo_half), q2, 0.0)
                s = _dot_nt(qm.astype(BF16), k2) + slopes[h] * nbias
                s = jnp.where(valid, s, -jnp.inf)
                m = jnp.max(s, axis=-1, keepdims=True)
                pe = jnp.exp(s - m)
                den = jnp.sum(pe, axis=-1, keepdims=True)
                outs.append(_dot(pe.astype(BF16), v2) / den)
                lse_tile = jnp.where(lane == h, m + jnp.log(den), lse_tile)
            o_ref[pl.ds(r0, QB), cs] = jnp.where(lo_half, outs[0], outs[1])
        lse_ref[pl.ds(r0, QB), :] = lse_tile
        return carry

    lax.fori_loop(0, TQ // QB, sub, 0)


def _attn_group(qkv, dil, slopes, *, TQ=512):
    B, _, L, D3 = qkv.shape
    D = D3 // 3
    R = ATT_RADIUS
    TQ = min(TQ, L)
    nq = L // TQ
    hb = TQ // R
    last_hb = L // R - 1
    main = lambda c: pl.BlockSpec((None, None, TQ, D), lambda b, r, n: (b, r, n, c))
    prev = lambda c: pl.BlockSpec((None, None, R, D),
                                  lambda b, r, n: (b, r, jnp.maximum(n * hb - 1, 0), c))
    nxt = lambda c: pl.BlockSpec((None, None, R, D),
                                 lambda b, r, n: (b, r, jnp.minimum((n + 1) * hb, last_hb), c))
    kern = functools.partial(_attn_kernel, dil=dil, L=L, TQ=TQ, slopes=slopes)
    return pl.pallas_call(
        kern,
        grid=(B, dil, nq),
        in_specs=[main(0), main(1), main(2), prev(1), nxt(1), prev(2), nxt(2)],
        out_specs=[pl.BlockSpec((None, None, TQ, D), lambda b, r, n: (b, r, n, 0)),
                   pl.BlockSpec((None, None, TQ, LANES), lambda b, r, n: (b, r, n, 0))],
        out_shape=[jax.ShapeDtypeStruct((B, dil, L, D), F32),
                   jax.ShapeDtypeStruct((B, dil, L, LANES), F32)],
        scratch_shapes=[pltpu.VMEM((TQ + 2 * R, D), BF16), pltpu.VMEM((TQ + 2 * R, D), BF16)],
        compiler_params=_params(("parallel", "parallel", "parallel")),
        name="dilated_attn",
    )(qkv, qkv, qkv, qkv, qkv, qkv, qkv)


def _attn_merge_kernel(o0_ref, o1_ref, o2_ref, l0_ref, l1_ref, l2_ref, x_ref, ex_ref, wo_ref,
                       out_ref, s1_ref, s2_ref, t1_ref, t2_ref, *, tm):
    n_chunks = s1_ref.shape[0]

    def interleave(src_ref, lse_src_ref, dst_ref, lse_dst_ref, dil):
        rows = tm // dil
        for r in range(dil):
            lse_dst_ref[pl.ds(r, rows, stride=dil), :] = lse_src_ref[r]
            for c in range(n_chunks):
                dst_ref[c, pl.ds(r, rows, stride=dil), :] = src_ref[r, :, c * LANES:(c + 1) * LANES]
        return jnp.concatenate([dst_ref[c] for c in range(n_chunks)], axis=1)

    o1 = interleave(o1_ref, l1_ref, s1_ref, t1_ref, ATT_DILATIONS[1])
    o2 = interleave(o2_ref, l2_ref, s2_ref, t2_ref, ATT_DILATIONS[2])
    la, lb, lc = l0_ref[0], t1_ref[...], t2_ref[...]
    m = jnp.maximum(jnp.maximum(la, lb), lc)
    ea, eb, ec = jnp.exp(la - m), jnp.exp(lb - m), jnp.exp(lc - m)
    den = ea + eb + ec
    ex = ex_ref[...]

    def expand(w):
        hi = w.astype(BF16)
        lo = (w - hi.astype(F32)).astype(BF16)
        return _dot(hi, ex) + _dot(lo, ex)

    o = expand(ea / den) * o0_ref[0] + expand(eb / den) * o1 + expand(ec / den) * o2
    out_ref[...] = x_ref[...] + _dot(o.astype(BF16), wo_ref[...])


def _attn_merge(outs, lses, x, expand_mat, w_out, *, tm=512):
    B, S, D = x.shape
    tm = min(tm, S)
    d0, d1, d2 = ATT_DILATIONS
    grp = lambda d, w: pl.BlockSpec((None, d, tm // d, w), lambda b, i: (b, 0, i, 0))
    row = pl.BlockSpec((None, tm, D), lambda b, i: (b, i, 0))
    return pl.pallas_call(
        functools.partial(_attn_merge_kernel, tm=tm),
        grid=(B, S // tm),
        in_specs=[grp(d0, D), grp(d1, D), grp(d2, D), grp(d0, LANES), grp(d1, LANES),
                  grp(d2, LANES), row,
                  pl.BlockSpec((LANES, D), lambda b, i: (0, 0)),
                  pl.BlockSpec((D, D), lambda b, i: (0, 0))],
        out_specs=row,
        out_shape=jax.ShapeDtypeStruct((B, S, D), F32),
        scratch_shapes=[pltpu.VMEM((D // LANES, tm, LANES), F32),
                        pltpu.VMEM((D // LANES, tm, LANES), F32),
                        pltpu.VMEM((tm, LANES), F32), pltpu.VMEM((tm, LANES), F32)],
        compiler_params=_params(("parallel", "parallel")),
        name="attn_merge",
    )(*outs, *lses, x, expand_mat, w_out)


def _alibi_slopes(n):
    return tuple(float(s) for s in np.asarray(2.0 ** (-8.0 * (np.arange(n) + 1) / n), np.float32))


def _trunk(x, p):
    depth = p["norm_mix"].shape[0]
    D = x.shape[-1]
    slopes = _alibi_slopes(ATT_HEADS)
    for i in range(depth):
        a = i // 2
        nw_mix = p["norm_mix"][i][None, :]
        nw_ffn = p["norm_ffn"][i][None, :]
        if i % 2 == 0:
            proj = _norm_proj(x, nw_mix, p["hg_w_in"][a])[:, 0]
            o_f, o_b = _hgrn_recurrence(proj, p["lbs"][i])
            x = _hgrn_out(o_f, o_b, proj, x, p["hg_norm_tiled"][a], p["hg_w_out"][a])
            x = _ffn(x, nw_ffn, p["ffn_w_gate"][a], p["ffn_w_up"][a], p["ffn_w_down"][a])
        else:
            outs, lses = [], []
            for g, dil in enumerate(ATT_DILATIONS):
                w_g = p["attn_w_in"][a][:, 3 * D * g:3 * D * (g + 1)]
                qkv = _norm_proj(x, nw_mix, w_g, dil=dil)
                o_g, lse_g = _attn_group(qkv, dil, slopes)
                outs.append(o_g)
                lses.append(lse_g)
            x = _attn_merge(outs, lses, x, p["expand_mat"], p["attn_w_out"][a])
            final_nw = p["norm_out"][None, :] if i == depth - 1 else None
            x = _moe(x, nw_ffn, p["moe_router_t"][a], p["rank_tri"], p["moe_w_gate"][a],
                     p["moe_w_up"][a], p["moe_w_down"][a], final_nw)
    return x


def kernel(x_prompt, x_sample, norm_mix, norm_ffn, norm_out, hg_w_in, hg_lower_bounds, hg_norm,
           hg_w_out, attn_w_in, attn_w_out, ffn_w_gate, ffn_w_up, ffn_w_down, moe_router,
           moe_w_gate, moe_w_up, moe_w_down):
    D = x_prompt.shape[-1]
    assert norm_mix.shape[0] % 2 == 0, "trunk ends on an attention/MoE layer"
    pr = jax.nn.softmax(hg_lower_bounds.astype(F32), axis=0)
    head_of_lane = np.arange(D) // ATT_HEAD_DIM
    expand_mat = jnp.asarray(np.arange(LANES)[:, None] == head_of_lane[None, :], BF16)
    p = dict(
        norm_mix=norm_mix, norm_ffn=norm_ffn, norm_out=norm_out,
        lbs=jnp.cumsum(pr, axis=0) - pr[0],
        hg_w_in=hg_w_in.astype(BF16), hg_w_out=hg_w_out.astype(BF16),
        hg_norm_tiled=jnp.tile(hg_norm.astype(F32), (1, D // HG_HEAD_DIM))[:, None, :],
        attn_w_in=attn_w_in.astype(BF16), attn_w_out=attn_w_out.astype(BF16),
        ffn_w_gate=ffn_w_gate.astype(BF16), ffn_w_up=ffn_w_up.astype(BF16),
        ffn_w_down=ffn_w_down.astype(BF16),
        moe_router_t=jnp.swapaxes(moe_router, 1, 2).astype(F32),
        moe_w_gate=moe_w_gate.astype(BF16), moe_w_up=moe_w_up.astype(BF16),
        moe_w_down=moe_w_down.astype(BF16),
        expand_mat=expand_mat,
        rank_tri=jnp.asarray(np.tri(ROUTER_TM, k=-1), BF16),
    )
    return (_trunk(x_prompt, p), _trunk(x_sample, p))
```

```python
import functools

import jax
import jax.numpy as jnp
import numpy as np
from jax import lax
from jax.experimental import pallas as pl
from jax.experimental.pallas import tpu as pltpu

F32 = jnp.float32
BF16 = jnp.bfloat16

EPS = 1e-6
HG_HEAD_DIM = 128
HG_CHUNK = 64
HG_SUB = 16
HG_EXP_CLAMP = 80.0
HG_UNROLL = 8
ATT_HEADS = 16
ATT_HEAD_DIM = 64
ATT_RADIUS = 64
ATT_DILATIONS = (1, 4, 16)
ATT_QB = 128
ATT_KW = ATT_QB + 2 * ATT_RADIUS
LOG2E = 1.4426950408889634
LN2 = 0.6931471805599453
ATT_Q_SCALE = ATT_HEAD_DIM ** -0.5 * LOG2E
N_EXPERTS = 8
LANES = 128
VMEM_LIMIT = 56 * 1024 * 1024


def _params(sem):
    return pltpu.CompilerParams(dimension_semantics=sem, vmem_limit_bytes=VMEM_LIMIT)


def _rms(x, w):
    return (x * lax.rsqrt(jnp.mean(x * x, axis=-1, keepdims=True) + EPS)) * w


def _sigmoid(x):
    return 1.0 / (1.0 + jnp.exp(-x))


def _dot(a, b):
    return jnp.dot(a, b, preferred_element_type=F32)


def _dot_nt(a, b):
    return lax.dot_general(a, b, (((1,), (1,)), ((), ())), preferred_element_type=F32)


def _dot_tn(a, b):
    return lax.dot_general(a, b, (((0,), (0,)), ((), ())), preferred_element_type=F32)


PROJ_TN = 1024


def _norm_proj_kernel(x_ref, nw_ref, w_ref, o_ref, h_ref, *xs_ref, dil, tm, first_scale):
    rows = tm // dil
    nw = nw_ref[...]
    if dil == 1:
        h_ref[...] = _rms(x_ref[...], nw).astype(BF16)
    else:
        xs = xs_ref[0]
        n_chunks = xs.shape[0]
        for c in range(n_chunks):
            xs[c] = x_ref[:, c * LANES:(c + 1) * LANES]
        for r in range(dil):
            xr = jnp.concatenate(
                [xs[c, pl.ds(r, rows, stride=dil), :] for c in range(n_chunks)], axis=1)
            h_ref[r * rows:(r + 1) * rows, :] = _rms(xr, nw).astype(BF16)

    h = h_ref[...]
    for j in range(w_ref.shape[1] // PROJ_TN):
        cols = slice(j * PROJ_TN, (j + 1) * PROJ_TN)
        acc = _dot(h, w_ref[:, cols])
        if j == 0 and first_scale is not None:
            acc = acc * first_scale
        for r in range(dil):
            o_ref[r, :, cols] = acc[r * rows:(r + 1) * rows].astype(o_ref.dtype)


def _norm_proj(x, nw, w, *, dil=1, tm=1024, first_scale=None, out_dtype=BF16):
    B, S, D = x.shape
    N = w.shape[1]
    tm = min(tm, S)
    kern = functools.partial(_norm_proj_kernel, dil=dil, tm=tm, first_scale=first_scale)
    return pl.pallas_call(
        kern,
        grid=(B, S // tm),
        in_specs=[
            pl.BlockSpec((None, tm, D), lambda b, i: (b, i, 0)),
            pl.BlockSpec((1, D), lambda b, i: (0, 0)),
            pl.BlockSpec((D, N), lambda b, i: (0, 0)),
        ],
        out_specs=pl.BlockSpec((None, dil, tm // dil, N), lambda b, i: (b, 0, i, 0)),
        out_shape=jax.ShapeDtypeStruct((B, dil, S // dil, N), out_dtype),
        scratch_shapes=[pltpu.VMEM((tm, D), BF16)]
        + ([pltpu.VMEM((D // LANES, tm, LANES), F32)] if dil > 1 else []),
        compiler_params=_params(("parallel", "parallel")),
        name="norm_proj",
    )(x, nw, w)


def _hgrn_gates(q, f_raw, lb, tri, rev):
    q = q.astype(F32)
    q = q * _sigmoid(q)
    f = lb + (1.0 - lb) * _sigmoid(f_raw.astype(F32))
    g = jnp.log(f)
    g1 = g.astype(BF16)
    d1 = g - g1.astype(F32)
    g2 = d1.astype(BF16)
    g3 = (d1 - g2.astype(F32)).astype(BF16)
    b = _dot(tri, g1) + _dot(tri, g2) + _dot(tri, g3)
    return q, 1.0 - f, b


def _hgrn_scores(q, k, v, b, rev):
    C, SUB = HG_CHUNK, HG_SUB
    b_tot = b[0:1, :] if rev else b[C - 1:C, :]
    q_inter = (q * jnp.exp(b)).astype(BF16)
    kk = (k * jnp.exp(b_tot - b)).astype(BF16)
    st_inc = _dot_tn(v, kk)
    scores = []
    for i in range(C // SUB):
        lo, hi = i * SUB, (i + 1) * SUB
        if rev:
            klo, khi, mid = lo, C, lo + SUB // 2
        else:
            klo, khi, mid = 0, hi, lo + SUB // 2 - 1
        mu = b[mid:mid + 1, :]
        qi = (q[lo:hi] * jnp.exp(jnp.minimum(b[lo:hi] - mu, HG_EXP_CLAMP))).astype(BF16)
        ki = (k[klo:khi] * jnp.exp(jnp.minimum(mu - b[klo:khi], HG_EXP_CLAMP))).astype(BF16)
        scores.append(_dot_nt(qi, ki))
    return q_inter, jnp.exp(b_tot), st_inc, scores


def _hgrn_outputs(scores, v, q_inter, st, rev):
    C, SUB = HG_CHUNK, HG_SUB
    outs = []
    for i, a in enumerate(scores):
        lo = i * SUB
        klo, khi = (lo, C) if rev else (0, lo + SUB)
        t_idx = lo + lax.broadcasted_iota(jnp.int32, a.shape, 0)
        s_idx = klo + lax.broadcasted_iota(jnp.int32, a.shape, 1)
        a = jnp.where(s_idx >= t_idx if rev else s_idx <= t_idx, a, 0.0)
        outs.append(_dot(a.astype(BF16), v[klo:khi]))
    return _dot_nt(q_inter, st.astype(BF16)) + jnp.concatenate(outs, axis=0)


def _hgrn_kernel(qf_ref, ff_ref, vf_ref, qb_ref, fb_ref, vb_ref, lb_ref, of_ref, ob_ref,
                 sf_ref, sb_ref, *, T):
    C, U = HG_CHUNK, HG_UNROLL
    span = C * U

    @pl.when(pl.program_id(2) == 0)
    def _():
        sf_ref[...] = jnp.zeros_like(sf_ref)
        sb_ref[...] = jnp.zeros_like(sb_ref)

    row = lax.broadcasted_iota(jnp.int32, (C, C), 0)
    col = lax.broadcasted_iota(jnp.int32, (C, C), 1)
    tri_f = jnp.where(col <= row, 1.0, 0.0).astype(BF16)
    tri_b = jnp.where(col >= row, 1.0, 0.0).astype(BF16)
    lbf = lb_ref[0:1, :]
    lbb = lb_ref[1:2, :]
    n_spans = T // span

    def body(it, carry):
        r_f = pl.multiple_of(it * span, span)
        r_b = pl.multiple_of((n_spans - 1 - it) * span, span)
        rows_f, rows_b = pl.ds(r_f, span), pl.ds(r_b, span)
        qf, ff, vf = qf_ref[rows_f, :], ff_ref[rows_f, :], vf_ref[rows_f, :]
        qb, fb, vb = qb_ref[rows_b, :], fb_ref[rows_b, :], vb_ref[rows_b, :]
        chains = []
        for u in range(U):
            cf = slice(u * C, (u + 1) * C)
            chains.append((qf[cf], ff[cf], vf[cf], lbf, tri_f, False))
            cb = slice((U - 1 - u) * C, (U - u) * C)
            chains.append((qb[cb], fb[cb], vb[cb], lbb, tri_b, True))
        gates = [_hgrn_gates(q, f, lb, tri, rev) for q, f, _, lb, tri, rev in chains]
        mids = [_hgrn_scores(q, k, ch[2], b, ch[5]) for (q, k, b), ch in zip(gates, chains)]
        st = {False: sf_ref[...], True: sb_ref[...]}
        outs = {False: [], True: []}
        for (q_inter, decay, st_inc, scores), ch in zip(mids, chains):
            rev = ch[5]
            outs[rev].append(_hgrn_outputs(scores, ch[2], q_inter, st[rev], rev))
            st[rev] = st[rev] * decay + st_inc
        of_ref[rows_f, :] = jnp.concatenate(outs[False], axis=0)
        ob_ref[rows_b, :] = jnp.concatenate(outs[True][::-1], axis=0)
        sf_ref[...] = st[False]
        sb_ref[...] = st[True]
        return carry

    lax.fori_loop(0, n_spans, body, 0)


def _hgrn_recurrence(proj, lb, *, T=2048):
    B, S, D5 = proj.shape
    D = D5 // 5
    H = D // HG_HEAD_DIM
    T = min(T, S)
    nT = S // T
    blk = (None, T, HG_HEAD_DIM)
    fwd = lambda off: pl.BlockSpec(blk, lambda b, h, t: (b, t, off + h))
    bwd = lambda off: pl.BlockSpec(blk, lambda b, h, t: (b, nT - 1 - t, off + h))
    out_shape = jax.ShapeDtypeStruct((B, S, D), F32)
    return pl.pallas_call(
        functools.partial(_hgrn_kernel, T=T),
        grid=(B, H, nT),
        in_specs=[fwd(0), fwd(H), fwd(3 * H), bwd(0), bwd(2 * H), bwd(3 * H),
                  pl.BlockSpec((2, HG_HEAD_DIM), lambda b, h, t: (0, h))],
        out_specs=[pl.BlockSpec(blk, lambda b, h, t: (b, t, h)),
                   pl.BlockSpec(blk, lambda b, h, t: (b, nT - 1 - t, h))],
        out_shape=[out_shape, out_shape],
        scratch_shapes=[pltpu.VMEM((HG_HEAD_DIM, HG_HEAD_DIM), F32),
                        pltpu.VMEM((HG_HEAD_DIM, HG_HEAD_DIM), F32)],
        compiler_params=_params(("parallel", "parallel", "arbitrary")),
        name="hgrn_recurrence",
    )(proj, proj, proj, proj, proj, proj, lb)


def _hgrn_out_kernel(of_ref, ob_ref, gate_ref, x_ref, wn_ref, wo_ref, out_ref):
    o = of_ref[...] + ob_ref[...]
    D = o.shape[-1]
    parts = []
    for h in range(D // HG_HEAD_DIM):
        oh = o[:, h * HG_HEAD_DIM:(h + 1) * HG_HEAD_DIM]
        parts.append(oh * lax.rsqrt(jnp.mean(oh * oh, axis=-1, keepdims=True) + EPS))
    gate = gate_ref[...].astype(F32)
    y = (jnp.concatenate(parts, axis=-1) * wn_ref[...]) * (gate * _sigmoid(gate))
    out_ref[...] = x_ref[...] + _dot(y.astype(BF16), wo_ref[...])


def _hgrn_out(o_f, o_b, proj, x, wn_tiled, w_out, *, tm=512):
    B, S, D = x.shape
    tm = min(tm, S)
    row = pl.BlockSpec((None, tm, D), lambda b, i: (b, i, 0))
    return pl.pallas_call(
        _hgrn_out_kernel,
        grid=(B, S // tm),
        in_specs=[row, row,
                  pl.BlockSpec((None, tm, D), lambda b, i: (b, i, 4)),
                  row,
                  pl.BlockSpec((1, D), lambda b, i: (0, 0)),
                  pl.BlockSpec((D, D), lambda b, i: (0, 0))],
        out_specs=row,
        out_shape=jax.ShapeDtypeStruct((B, S, D), F32),
        compiler_params=_params(("parallel", "parallel")),
        name="hgrn_out",
    )(o_f, o_b, proj, x, wn_tiled, w_out)


def _swiglu_step(h_ref, wg_ref, wu_ref, wd_ref, acc_ref):
    h = h_ref[...]
    g = _dot(h, wg_ref[...])
    u = _dot(h, wu_ref[...])
    a = (g * _sigmoid(g)) * u
    acc_ref[...] += _dot(a.astype(BF16), wd_ref[...])


def _ffn_kernel(x_ref, nw_ref, wg_ref, wu_ref, wd_ref, o_ref, h_ref, acc_ref):
    j = pl.program_id(2)

    @pl.when(j == 0)
    def _():
        h_ref[...] = _rms(x_ref[...], nw_ref[...]).astype(BF16)
        acc_ref[...] = jnp.zeros_like(acc_ref)

    _swiglu_step(h_ref, wg_ref, wu_ref, wd_ref, acc_ref)

    @pl.when(j == pl.num_programs(2) - 1)
    def _():
        o_ref[...] = x_ref[...] + acc_ref[...]


def _ffn(x, nw, w_gate, w_up, w_down, *, tm=512, tf=1792):
    B, S, D = x.shape
    F = w_gate.shape[-1]
    tm = min(tm, S)
    row = pl.BlockSpec((None, tm, D), lambda b, i, j: (b, i, 0))
    return pl.pallas_call(
        _ffn_kernel,
        grid=(B, S // tm, F // tf),
        in_specs=[row,
                  pl.BlockSpec((1, D), lambda b, i, j: (0, 0)),
                  pl.BlockSpec((D, tf), lambda b, i, j: (0, j)),
                  pl.BlockSpec((D, tf), lambda b, i, j: (0, j)),
                  pl.BlockSpec((tf, D), lambda b, i, j: (j, 0))],
        out_specs=row,
        out_shape=jax.ShapeDtypeStruct((B, S, D), F32),
        scratch_shapes=[pltpu.VMEM((tm, D), BF16), pltpu.VMEM((tm, D), F32)],
        compiler_params=_params(("parallel", "parallel", "arbitrary")),
        name="swiglu_ffn",
    )(x, nw, w_gate, w_up, w_down)


MOE_RB = 512
MOE_TF = 1792
ROUTER_TM = 512
DISPATCH_TM = 512
COMBINE_TM = 256


def _router_kernel(x_ref, nw_ref, wr_ref, tri_ref, w_ref, i_ref, cnt_ref, run_ref):
    @pl.when(pl.program_id(0) == 0)
    def _():
        run_ref[...] = jnp.zeros_like(run_ref)

    h = _rms(x_ref[...], nw_ref[...])
    logits = [jnp.sum(h * wr_ref[e:e + 1, :], axis=-1, keepdims=True) for e in range(N_EXPERTS)]

    def top(vals):
        m = functools.reduce(jnp.maximum, vals)
        idx = functools.reduce(
            jnp.minimum, [jnp.where(v == m, e, N_EXPERTS) for e, v in enumerate(vals)])
        return m, idx

    m1, i1 = top(logits)
    m2, i2 = top([jnp.where(i1 == e, -jnp.inf, v) for e, v in enumerate(logits)])
    e2 = jnp.exp(m2 - m1)
    w1 = 1.0 / (1.0 + e2)
    w2 = e2 / (1.0 + e2)

    tm = h.shape[0]
    lane = lax.broadcasted_iota(jnp.int32, (tm, LANES), 1)
    oh1 = jnp.where(lane == i1, 1.0, 0.0)
    oh2 = jnp.where(lane == i2, 1.0, 0.0)
    oh = oh1 + oh2
    before = _dot(tri_ref[...], oh.astype(BF16)) + run_ref[...]
    r1 = jnp.sum(oh1 * before, axis=-1, keepdims=True).astype(jnp.int32)
    r2 = jnp.sum(oh2 * before, axis=-1, keepdims=True).astype(jnp.int32)
    run_ref[...] += jnp.sum(oh, axis=0, keepdims=True)
    cnt_ref[...] = run_ref[...].astype(jnp.int32)
    w_ref[...] = jnp.where(lane == 0, w1, jnp.where(lane == 1, w2, 0.0))
    i_ref[...] = jnp.where(lane == 0, i1, jnp.where(lane == 1, i2,
                           jnp.where(lane == 2, r1, jnp.where(lane == 3, r2, 0))))


def _router(x2, nw, w_router_t, tri, *, tm=ROUTER_TM):
    N, D = x2.shape
    row = lambda w: pl.BlockSpec((tm, w), lambda i: (i, 0))
    return pl.pallas_call(
        _router_kernel,
        grid=(N // tm,),
        in_specs=[row(D),
                  pl.BlockSpec((1, D), lambda i: (0, 0)),
                  pl.BlockSpec((N_EXPERTS, D), lambda i: (0, 0)),
                  pl.BlockSpec((tm, tm), lambda i: (0, 0))],
        out_specs=[row(LANES), row(LANES), pl.BlockSpec((1, LANES), lambda i: (0, 0))],
        out_shape=[jax.ShapeDtypeStruct((N, LANES), F32),
                   jax.ShapeDtypeStruct((N, LANES), jnp.int32),
                   jax.ShapeDtypeStruct((1, LANES), jnp.int32)],
        scratch_shapes=[pltpu.VMEM((1, LANES), F32)],
        compiler_params=_params(("arbitrary",)),
        name="moe_router",
    )(x2, nw, w_router_t, tri)


def _row_copy(src_hbm, src_row, dst_ref, dst_row, sem):
    return pltpu.make_async_copy(src_hbm.at[pl.ds(src_row, 1)], dst_ref.at[pl.ds(dst_row, 1)], sem)


def _dispatch_kernel(dst_ref, x_ref, xs_init_hbm, xs_hbm, sem, *, tm):
    del xs_init_hbm

    def issue(t, carry):
        _row_copy(x_ref, t, xs_hbm, dst_ref[0, 2 * t], sem).start()
        _row_copy(x_ref, t, xs_hbm, dst_ref[0, 2 * t + 1], sem).start()
        return carry

    lax.fori_loop(0, tm, issue, 0, unroll=8)
    for _ in range(2):
        pltpu.make_async_copy(x_ref, xs_hbm.at[pl.ds(0, tm)], sem).wait()


def _dispatch(x2, dst, n_rows, *, tm=DISPATCH_TM):
    N, D = x2.shape
    return pl.pallas_call(
        functools.partial(_dispatch_kernel, tm=tm),
        grid=(N // tm,),
        in_specs=[pl.BlockSpec((None, 1, 2 * tm), lambda i: (i, 0, 0), memory_space=pltpu.SMEM),
                  pl.BlockSpec((tm, D), lambda i: (i, 0)),
                  pl.BlockSpec(memory_space=pl.ANY)],
        out_specs=pl.BlockSpec(memory_space=pl.ANY),
        out_shape=jax.ShapeDtypeStruct((n_rows, D), F32),
        scratch_shapes=[pltpu.SemaphoreType.DMA(())],
        input_output_aliases={2: 0},
        compiler_params=_params(("arbitrary",)),
        name="moe_dispatch",
    )(dst.reshape(N // tm, 1, 2 * tm), x2, jnp.zeros((n_rows, D), F32))


def _expert_kernel(be_ref, nb_ref, x_ref, nw_ref, wg_ref, wu_ref, wd_ref, o_ref, h_ref, acc_ref):
    del be_ref
    j = pl.program_id(1)
    used = pl.program_id(0) < nb_ref[0]

    @pl.when(jnp.logical_and(jnp.logical_not(used), j == 0))
    def _():
        o_ref[...] = jnp.zeros_like(o_ref)

    @pl.when(used)
    def _():
        @pl.when(j == 0)
        def _():
            h_ref[...] = _rms(x_ref[...], nw_ref[...]).astype(BF16)
            acc_ref[...] = jnp.zeros_like(acc_ref)

        _swiglu_step(h_ref, wg_ref, wu_ref, wd_ref, acc_ref)

        @pl.when(j == pl.num_programs(1) - 1)
        def _():
            o_ref[...] = acc_ref[...]


def _experts(xs, nw, blk_expert, n_blk, w_gate, w_up, w_down, *, rb=MOE_RB, tf=MOE_TF):
    R, D = xs.shape
    F = w_gate.shape[-1]
    nj = F // tf
    used = lambda r, nb: jnp.minimum(r, nb[0] - 1)
    jj = lambda r, j, nb: jnp.where(r < nb[0], j, nj - 1)
    row = pl.BlockSpec((rb, D), lambda r, j, be, nb: (used(r, nb), 0))
    w_in = pl.BlockSpec((None, D, tf), lambda r, j, be, nb: (be[used(r, nb)], 0, jj(r, j, nb)))
    w_out = pl.BlockSpec((None, tf, D), lambda r, j, be, nb: (be[used(r, nb)], jj(r, j, nb), 0))
    return pl.pallas_call(
        _expert_kernel,
        grid_spec=pltpu.PrefetchScalarGridSpec(
            num_scalar_prefetch=2,
            grid=(R // rb, nj),
            in_specs=[row, pl.BlockSpec((1, D), lambda r, j, be, nb: (0, 0)), w_in, w_in, w_out],
            out_specs=pl.BlockSpec((rb, D), lambda r, j, be, nb: (r, 0)),
            scratch_shapes=[pltpu.VMEM((rb, D), BF16), pltpu.VMEM((rb, D), F32)]),
        out_shape=jax.ShapeDtypeStruct((R, D), F32),
        compiler_params=_params(("arbitrary", "arbitrary")),
        name="moe_experts",
    )(blk_expert, n_blk, xs, nw, w_gate, w_up, w_down)


def _combine_kernel(*refs, tm, final_norm):
    if final_norm:
        cur_ref, nxt_ref, w_ref, x_ref, y_hbm, fnw_ref, o_ref, ybuf, sem = refs
    else:
        cur_ref, nxt_ref, w_ref, x_ref, y_hbm, o_ref, ybuf, sem = refs
    i = pl.program_id(0)
    slot = i % 2

    def issue(idx_ref, s):
        def body(t, carry):
            _row_copy(y_hbm, idx_ref[0, 2 * t], ybuf.at[s], t, sem.at[s]).start()
            _row_copy(y_hbm, idx_ref[0, 2 * t + 1], ybuf.at[s], tm + t, sem.at[s]).start()
            return carry
        lax.fori_loop(0, tm, body, 0, unroll=8)

    @pl.when(i == 0)
    def _():
        issue(cur_ref, 0)

    @pl.when(i + 1 < pl.num_programs(0))
    def _():
        issue(nxt_ref, 1 - slot)

    pltpu.make_async_copy(y_hbm.at[pl.ds(0, 2 * tm)], ybuf.at[slot], sem.at[slot]).wait()
    w = w_ref[...]
    y = ybuf[slot]
    out = x_ref[...] + w[:, 0:1] * y[0:tm] + w[:, 1:2] * y[tm:2 * tm]
    if final_norm:
        out = _rms(out, fnw_ref[...])
    o_ref[...] = out


def _combine(x2, y, dst, wts, final_nw, *, tm=COMBINE_TM):
    N, D = x2.shape
    nt = N // tm
    idx = dst.reshape(nt, 1, 2 * tm)
    smem = lambda f: pl.BlockSpec((None, 1, 2 * tm), f, memory_space=pltpu.SMEM)
    in_specs = [smem(lambda i: (i, 0, 0)),
                smem(lambda i: (jnp.minimum(i + 1, nt - 1), 0, 0)),
                pl.BlockSpec((tm, LANES), lambda i: (i, 0)),
                pl.BlockSpec((tm, D), lambda i: (i, 0)),
                pl.BlockSpec(memory_space=pl.ANY)]
    args = [idx, idx, wts, x2, y]
    if final_nw is not None:
        in_specs.append(pl.BlockSpec((1, D), lambda i: (0, 0)))
        args.append(final_nw)
    return pl.pallas_call(
        functools.partial(_combine_kernel, tm=tm, final_norm=final_nw is not None),
        grid=(nt,),
        in_specs=in_specs,
        out_specs=pl.BlockSpec((tm, D), lambda i: (i, 0)),
        out_shape=jax.ShapeDtypeStruct((N, D), F32),
        scratch_shapes=[pltpu.VMEM((2, 2 * tm, D), F32), pltpu.SemaphoreType.DMA((2,))],
        compiler_params=_params(("arbitrary",)),
        name="moe_combine",
    )(*args)


def _moe(x, nw, w_router_t, tri, w_gate, w_up, w_down, final_nw):
    B, S, D = x.shape
    N = B * S
    rb = MOE_RB
    x2 = x.reshape(N, D)
    wts, meta, cnt = _router(x2, nw, w_router_t, tri)
    counts = cnt[0, :N_EXPERTS]
    group = ((counts + rb - 1) // rb) * rb
    ends = jnp.cumsum(group)
    offs = ends - group
    dst = jnp.stack([offs[meta[:, 0]] + meta[:, 2], offs[meta[:, 1]] + meta[:, 3]], axis=-1)
    n_blocks = (2 * N) // rb + N_EXPERTS
    blk_expert = jnp.minimum(
        jnp.searchsorted(ends // rb, jnp.arange(n_blocks, dtype=jnp.int32), side="right"),
        N_EXPERTS - 1).astype(jnp.int32)
    n_blk = (ends[-1:] // rb).astype(jnp.int32)
    xs = _dispatch(x2, dst, n_blocks * rb)
    y = _experts(xs, nw, blk_expert, n_blk, w_gate, w_up, w_down)
    return _combine(x2, y, dst, wts, final_nw).reshape(B, S, D)


def _attn_kernel(q_ref, k_ref, v_ref, kp_ref, kn_ref, vp_ref, vn_ref, bias_ref, o_ref, lse_ref,
                 kw_ref, vw_ref, *, L, TQ):
    R, QB, KW = ATT_RADIUS, ATT_QB, ATT_KW
    n = pl.program_id(2)
    kw_ref[0:R, :] = kp_ref[...]
    kw_ref[R:R + TQ, :] = k_ref[...]
    kw_ref[R + TQ:, :] = kn_ref[...]
    vw_ref[0:R, :] = vp_ref[...]
    vw_ref[R:R + TQ, :] = v_ref[...]
    vw_ref[R + TQ:, :] = vn_ref[...]

    kcol = lax.broadcasted_iota(jnp.int32, (QB, KW), 1)
    lane = lax.broadcasted_iota(jnp.int32, (QB, LANES), 1)
    lo_half = lane < ATT_HEAD_DIM

    def sub_block(i, at_edge):
        r0 = pl.multiple_of(i * QB, QB)
        if at_edge:
            kpos = n * TQ + i * QB - R + kcol
            in_seq = (kpos >= 0) & (kpos < L)
        lse_tile = jnp.zeros((QB, LANES), F32)
        for p in range(ATT_HEADS // 2):
            cs = slice(p * LANES, (p + 1) * LANES)
            q2 = q_ref[pl.ds(r0, QB), cs]
            k2 = kw_ref[pl.ds(r0, KW), cs]
            v2 = vw_ref[pl.ds(r0, KW), cs]
            outs = []
            for hh in range(2):
                h = 2 * p + hh
                qm = jnp.where(lo_half if hh == 0 else jnp.logical_not(lo_half), q2, 0.0)
                s = _dot_nt(qm.astype(BF16), k2) + bias_ref[h]
                if at_edge:
                    s = jnp.where(in_seq, s, -jnp.inf)
                m = jnp.max(s, axis=-1, keepdims=True)
                pe = jnp.exp2(s - m)
                den = jnp.sum(pe, axis=-1, keepdims=True)
                outs.append(_dot(pe.astype(BF16), v2) / den)
                lse_tile = jnp.where(lane == h, LN2 * (m + jnp.log2(den)), lse_tile)
            o_ref[pl.ds(r0, QB), cs] = jnp.where(lo_half, outs[0], outs[1])
        lse_ref[pl.ds(r0, QB), :] = lse_tile

    def sub(i, carry):
        q0 = n * TQ + i * QB
        at_edge = jnp.logical_or(q0 == 0, q0 + QB == L)
        pl.when(at_edge)(lambda: sub_block(i, True))
        pl.when(jnp.logical_not(at_edge))(lambda: sub_block(i, False))
        return carry

    lax.fori_loop(0, TQ // QB, sub, 0)


def _attn_bias(dil, slopes):
    rel = np.arange(ATT_KW)[None, :] - ATT_RADIUS - np.arange(ATT_QB)[:, None]
    arel = np.abs(rel).astype(np.float32)
    bias = -np.asarray(slopes, np.float32)[:, None, None] * (np.float32(dil) * arel)[None]
    bias = np.where((arel <= ATT_RADIUS)[None], bias * np.float32(LOG2E), -np.inf)
    return jnp.asarray(bias, F32)


def _attn_group(qkv, dil, slopes, *, TQ=512):
    B, _, L, D3 = qkv.shape
    D = D3 // 3
    R = ATT_RADIUS
    TQ = min(TQ, L)
    nq = L // TQ
    hb = TQ // R
    last_hb = L // R - 1
    main = lambda c: pl.BlockSpec((None, None, TQ, D), lambda b, r, n: (b, r, n, c))
    prev = lambda c: pl.BlockSpec((None, None, R, D),
                                  lambda b, r, n: (b, r, jnp.maximum(n * hb - 1, 0), c))
    nxt = lambda c: pl.BlockSpec((None, None, R, D),
                                 lambda b, r, n: (b, r, jnp.minimum((n + 1) * hb, last_hb), c))
    kern = functools.partial(_attn_kernel, L=L, TQ=TQ)
    return pl.pallas_call(
        kern,
        grid=(B, dil, nq),
        in_specs=[main(0), main(1), main(2), prev(1), nxt(1), prev(2), nxt(2),
                  pl.BlockSpec((ATT_HEADS, ATT_QB, ATT_KW), lambda b, r, n: (0, 0, 0))],
        out_specs=[pl.BlockSpec((None, None, TQ, D), lambda b, r, n: (b, r, n, 0)),
                   pl.BlockSpec((None, None, TQ, LANES), lambda b, r, n: (b, r, n, 0))],
        out_shape=[jax.ShapeDtypeStruct((B, dil, L, D), F32),
                   jax.ShapeDtypeStruct((B, dil, L, LANES), F32)],
        scratch_shapes=[pltpu.VMEM((TQ + 2 * R, D), BF16), pltpu.VMEM((TQ + 2 * R, D), BF16)],
        compiler_params=_params(("parallel", "parallel", "parallel")),
        name="dilated_attn",
    )(qkv, qkv, qkv, qkv, qkv, qkv, qkv, _attn_bias(dil, slopes))


def _attn_merge_kernel(o0_ref, o1_ref, o2_ref, l0_ref, l1_ref, l2_ref, x_ref, ex_ref, wo_ref,
                       out_ref, s1_ref, s2_ref, t1_ref, t2_ref, *, tm):
    n_chunks = s1_ref.shape[0]

    def interleave(src_ref, lse_src_ref, dst_ref, lse_dst_ref, dil):
        rows = tm // dil
        for r in range(dil):
            lse_dst_ref[pl.ds(r, rows, stride=dil), :] = lse_src_ref[r]
            for c in range(n_chunks):
                dst_ref[c, pl.ds(r, rows, stride=dil), :] = src_ref[r, :, c * LANES:(c + 1) * LANES]
        return jnp.concatenate([dst_ref[c] for c in range(n_chunks)], axis=1)

    o1 = interleave(o1_ref, l1_ref, s1_ref, t1_ref, ATT_DILATIONS[1])
    o2 = interleave(o2_ref, l2_ref, s2_ref, t2_ref, ATT_DILATIONS[2])
    la, lb, lc = l0_ref[0], t1_ref[...], t2_ref[...]
    m = jnp.maximum(jnp.maximum(la, lb), lc)
    ea, eb, ec = jnp.exp(la - m), jnp.exp(lb - m), jnp.exp(lc - m)
    den = ea + eb + ec
    ex = ex_ref[...]

    def expand(w):
        hi = w.astype(BF16)
        lo = (w - hi.astype(F32)).astype(BF16)
        return _dot(hi, ex) + _dot(lo, ex)

    o = expand(ea / den) * o0_ref[0] + expand(eb / den) * o1 + expand(ec / den) * o2
    out_ref[...] = x_ref[...] + _dot(o.astype(BF16), wo_ref[...])


def _attn_merge(outs, lses, x, expand_mat, w_out, *, tm=512):
    B, S, D = x.shape
    tm = min(tm, S)
    d0, d1, d2 = ATT_DILATIONS
    grp = lambda d, w: pl.BlockSpec((None, d, tm // d, w), lambda b, i: (b, 0, i, 0))
    row = pl.BlockSpec((None, tm, D), lambda b, i: (b, i, 0))
    return pl.pallas_call(
        functools.partial(_attn_merge_kernel, tm=tm),
        grid=(B, S // tm),
        in_specs=[grp(d0, D), grp(d1, D), grp(d2, D), grp(d0, LANES), grp(d1, LANES),
                  grp(d2, LANES), row,
                  pl.BlockSpec((LANES, D), lambda b, i: (0, 0)),
                  pl.BlockSpec((D, D), lambda b, i: (0, 0))],
        out_specs=row,
        out_shape=jax.ShapeDtypeStruct((B, S, D), F32),
        scratch_shapes=[pltpu.VMEM((D // LANES, tm, LANES), F32),
                        pltpu.VMEM((D // LANES, tm, LANES), F32),
                        pltpu.VMEM((tm, LANES), F32), pltpu.VMEM((tm, LANES), F32)],
        compiler_params=_params(("parallel", "parallel")),
        name="attn_merge",
    )(*outs, *lses, x, expand_mat, w_out)


def _alibi_slopes(n):
    return tuple(float(s) for s in np.asarray(2.0 ** (-8.0 * (np.arange(n) + 1) / n), np.float32))


def _trunk(x, p):
    depth = p["norm_mix"].shape[0]
    D = x.shape[-1]
    slopes = _alibi_slopes(ATT_HEADS)
    for i in range(depth):
        a = i // 2
        nw_mix = p["norm_mix"][i][None, :]
        nw_ffn = p["norm_ffn"][i][None, :]
        if i % 2 == 0:
            proj = _norm_proj(x, nw_mix, p["hg_w_in"][a], tm=512)[:, 0]
            o_f, o_b = _hgrn_recurrence(proj, p["lbs"][i])
            x = _hgrn_out(o_f, o_b, proj, x, p["hg_norm_tiled"][a], p["hg_w_out"][a])
            x = _ffn(x, nw_ffn, p["ffn_w_gate"][a], p["ffn_w_up"][a], p["ffn_w_down"][a])
        else:
            outs, lses = [], []
            for g, dil in enumerate(ATT_DILATIONS):
                w_g = p["attn_w_in"][a][:, 3 * D * g:3 * D * (g + 1)]
                assert D == PROJ_TN
                qkv = _norm_proj(x, nw_mix, w_g, dil=dil, first_scale=ATT_Q_SCALE)
                o_g, lse_g = _attn_group(qkv, dil, slopes)
                outs.append(o_g)
                lses.append(lse_g)
            x = _attn_merge(outs, lses, x, p["expand_mat"], p["attn_w_out"][a])
            final_nw = p["norm_out"][None, :] if i == depth - 1 else None
            x = _moe(x, nw_ffn, p["moe_router_t"][a], p["rank_tri"], p["moe_w_gate"][a],
                     p["moe_w_up"][a], p["moe_w_down"][a], final_nw)
    return x


def kernel(x_prompt, x_sample, norm_mix, norm_ffn, norm_out, hg_w_in, hg_lower_bounds, hg_norm,
           hg_w_out, attn_w_in, attn_w_out, ffn_w_gate, ffn_w_up, ffn_w_down, moe_router,
           moe_w_gate, moe_w_up, moe_w_down):
    D = x_prompt.shape[-1]
    assert norm_mix.shape[0] % 2 == 0, "trunk ends on an attention/MoE layer"
    pr = jax.nn.softmax(hg_lower_bounds.astype(F32), axis=0)
    head_of_lane = np.arange(D) // ATT_HEAD_DIM
    expand_mat = jnp.asarray(np.arange(LANES)[:, None] == head_of_lane[None, :], BF16)
    p = dict(
        norm_mix=norm_mix, norm_ffn=norm_ffn, norm_out=norm_out,
        lbs=jnp.cumsum(pr, axis=0) - pr[0],
        hg_w_in=hg_w_in.astype(BF16), hg_w_out=hg_w_out.astype(BF16),
        hg_norm_tiled=jnp.tile(hg_norm.astype(F32), (1, D // HG_HEAD_DIM))[:, None, :],
        attn_w_in=attn_w_in.astype(BF16), attn_w_out=attn_w_out.astype(BF16),
        ffn_w_gate=ffn_w_gate.astype(BF16), ffn_w_up=ffn_w_up.astype(BF16),
        ffn_w_down=ffn_w_down.astype(BF16),
        moe_router_t=jnp.swapaxes(moe_router, 1, 2).astype(F32),
        moe_w_gate=moe_w_gate.astype(BF16), moe_w_up=moe_w_up.astype(BF16),
        moe_w_down=moe_w_down.astype(BF16),
        expand_mat=expand_mat,
        rank_tri=jnp.asarray(np.tri(ROUTER_TM, k=-1), BF16),
    )
    return (_trunk(x_prompt, p), _trunk(x_sample, p))
```

```python
import functools

import jax
import jax.numpy as jnp
import numpy as np
from jax import lax
from jax.experimental import pallas as pl
from jax.experimental.pallas import tpu as pltpu

F32 = jnp.float32
BF16 = jnp.bfloat16

EPS = 1e-6
HG_HEAD_DIM = 128
HG_CHUNK = 64
HG_SUB = 16
HG_EXP_CLAMP = 80.0
HG_UNROLL = 8
ATT_HEADS = 16
ATT_HEAD_DIM = 64
ATT_RADIUS = 64
ATT_DILATIONS = (1, 4, 16)
ATT_QB = 128
ATT_KW = ATT_QB + 2 * ATT_RADIUS
LOG2E = 1.4426950408889634
LN2 = 0.6931471805599453
ATT_Q_SCALE = ATT_HEAD_DIM ** -0.5 * LOG2E
N_EXPERTS = 8
LANES = 128
VMEM_LIMIT = 56 * 1024 * 1024


def _params(sem):
    return pltpu.CompilerParams(dimension_semantics=sem, vmem_limit_bytes=VMEM_LIMIT)


def _rms(x, w):
    return (x * lax.rsqrt(jnp.mean(x * x, axis=-1, keepdims=True) + EPS)) * w


def _sigmoid(x):
    return 1.0 / (1.0 + jnp.exp(-x))


def _dot(a, b):
    return jnp.dot(a, b, preferred_element_type=F32)


def _dot_nt(a, b):
    return lax.dot_general(a, b, (((1,), (1,)), ((), ())), preferred_element_type=F32)


def _dot_tn(a, b):
    return lax.dot_general(a, b, (((0,), (0,)), ((), ())), preferred_element_type=F32)


PROJ_TN = 1024


def _norm_proj_kernel(x_ref, nw_ref, w_ref, o_ref, h_ref, *xs_ref, dil, tm, first_scale):
    rows = tm // dil
    nw = nw_ref[...]
    if dil == 1:
        h_ref[...] = _rms(x_ref[...], nw).astype(BF16)
    else:
        xs = xs_ref[0]
        n_chunks = xs.shape[0]
        for c in range(n_chunks):
            xs[c] = x_ref[:, c * LANES:(c + 1) * LANES]
        for r in range(dil):
            xr = jnp.concatenate(
                [xs[c, pl.ds(r, rows, stride=dil), :] for c in range(n_chunks)], axis=1)
            h_ref[r * rows:(r + 1) * rows, :] = _rms(xr, nw).astype(BF16)

    h = h_ref[...]
    for j in range(w_ref.shape[1] // PROJ_TN):
        cols = slice(j * PROJ_TN, (j + 1) * PROJ_TN)
        acc = _dot(h, w_ref[:, cols])
        if j == 0 and first_scale is not None:
            acc = acc * first_scale
        for r in range(dil):
            o_ref[r, :, cols] = acc[r * rows:(r + 1) * rows].astype(o_ref.dtype)


def _norm_proj(x, nw, w, *, dil=1, tm=1024, first_scale=None, out_dtype=BF16):
    B, S, D = x.shape
    N = w.shape[1]
    tm = min(tm, S)
    kern = functools.partial(_norm_proj_kernel, dil=dil, tm=tm, first_scale=first_scale)
    return pl.pallas_call(
        kern,
        grid=(B, S // tm),
        in_specs=[
            pl.BlockSpec((None, tm, D), lambda b, i: (b, i, 0)),
            pl.BlockSpec((1, D), lambda b, i: (0, 0)),
            pl.BlockSpec((D, N), lambda b, i: (0, 0)),
        ],
        out_specs=pl.BlockSpec((None, dil, tm // dil, N), lambda b, i: (b, 0, i, 0)),
        out_shape=jax.ShapeDtypeStruct((B, dil, S // dil, N), out_dtype),
        scratch_shapes=[pltpu.VMEM((tm, D), BF16)]
        + ([pltpu.VMEM((D // LANES, tm, LANES), F32)] if dil > 1 else []),
        compiler_params=_params(("parallel", "parallel")),
        name="norm_proj",
    )(x, nw, w)


def _hgrn_gates(q, f_raw, lb, tri, rev):
    q = q.astype(F32)
    q = q * _sigmoid(q)
    f = lb + (1.0 - lb) * _sigmoid(f_raw.astype(F32))
    g = jnp.log(f)
    g1 = g.astype(BF16)
    d1 = g - g1.astype(F32)
    g2 = d1.astype(BF16)
    g3 = (d1 - g2.astype(F32)).astype(BF16)
    b = _dot(tri, g1) + _dot(tri, g2) + _dot(tri, g3)
    return q, 1.0 - f, b


def _hgrn_scores(q, k, v, b, rev):
    C, SUB = HG_CHUNK, HG_SUB
    b_tot = b[0:1, :] if rev else b[C - 1:C, :]
    q_inter = (q * jnp.exp(b)).astype(BF16)
    kk = (k * jnp.exp(b_tot - b)).astype(BF16)
    st_inc = _dot_tn(v, kk)
    scores = []
    for i in range(C // SUB):
        lo, hi = i * SUB, (i + 1) * SUB
        if rev:
            klo, khi, mid = lo, C, lo + SUB // 2
        else:
            klo, khi, mid = 0, hi, lo + SUB // 2 - 1
        mu = b[mid:mid + 1, :]
        qi = (q[lo:hi] * jnp.exp(jnp.minimum(b[lo:hi] - mu, HG_EXP_CLAMP))).astype(BF16)
        ki = (k[klo:khi] * jnp.exp(jnp.minimum(mu - b[klo:khi], HG_EXP_CLAMP))).astype(BF16)
        scores.append(_dot_nt(qi, ki))
    return q_inter, jnp.exp(b_tot), st_inc, scores


def _hgrn_outputs(scores, v, q_inter, st, rev):
    C, SUB = HG_CHUNK, HG_SUB
    outs = []
    for i, a in enumerate(scores):
        lo = i * SUB
        klo, khi = (lo, C) if rev else (0, lo + SUB)
        t_idx = lo + lax.broadcasted_iota(jnp.int32, a.shape, 0)
        s_idx = klo + lax.broadcasted_iota(jnp.int32, a.shape, 1)
        a = jnp.where(s_idx >= t_idx if rev else s_idx <= t_idx, a, 0.0)
        outs.append(_dot(a.astype(BF16), v[klo:khi]))
    return _dot_nt(q_inter, st.astype(BF16)) + jnp.concatenate(outs, axis=0)


def _hgrn_kernel(qf_ref, ff_ref, vf_ref, qb_ref, fb_ref, vb_ref, lb_ref, of_ref, ob_ref,
                 sf_ref, sb_ref, *, T):
    C, U = HG_CHUNK, HG_UNROLL
    span = C * U

    @pl.when(pl.program_id(2) == 0)
    def _():
        sf_ref[...] = jnp.zeros_like(sf_ref)
        sb_ref[...] = jnp.zeros_like(sb_ref)

    row = lax.broadcasted_iota(jnp.int32, (C, C), 0)
    col = lax.broadcasted_iota(jnp.int32, (C, C), 1)
    tri_f = jnp.where(col <= row, 1.0, 0.0).astype(BF16)
    tri_b = jnp.where(col >= row, 1.0, 0.0).astype(BF16)
    lbf = lb_ref[0:1, :]
    lbb = lb_ref[1:2, :]
    n_spans = T // span

    def body(it, carry):
        r_f = pl.multiple_of(it * span, span)
        r_b = pl.multiple_of((n_spans - 1 - it) * span, span)
        rows_f, rows_b = pl.ds(r_f, span), pl.ds(r_b, span)
        qf, ff, vf = qf_ref[rows_f, :], ff_ref[rows_f, :], vf_ref[rows_f, :]
        qb, fb, vb = qb_ref[rows_b, :], fb_ref[rows_b, :], vb_ref[rows_b, :]
        chains = []
        for u in range(U):
            cf = slice(u * C, (u + 1) * C)
            chains.append((qf[cf], ff[cf], vf[cf], lbf, tri_f, False))
            cb = slice((U - 1 - u) * C, (U - u) * C)
            chains.append((qb[cb], fb[cb], vb[cb], lbb, tri_b, True))
        gates = [_hgrn_gates(q, f, lb, tri, rev) for q, f, _, lb, tri, rev in chains]
        mids = [_hgrn_scores(q, k, ch[2], b, ch[5]) for (q, k, b), ch in zip(gates, chains)]
        st = {False: sf_ref[...], True: sb_ref[...]}
        outs = {False: [], True: []}
        for (q_inter, decay, st_inc, scores), ch in zip(mids, chains):
            rev = ch[5]
            outs[rev].append(_hgrn_outputs(scores, ch[2], q_inter, st[rev], rev))
            st[rev] = st[rev] * decay + st_inc
        of_ref[rows_f, :] = jnp.concatenate(outs[False], axis=0)
        ob_ref[rows_b, :] = jnp.concatenate(outs[True][::-1], axis=0)
        sf_ref[...] = st[False]
        sb_ref[...] = st[True]
        return carry

    lax.fori_loop(0, n_spans, body, 0)


def _hgrn_recurrence(proj, lb, *, T=2048):
    B, S, D5 = proj.shape
    D = D5 // 5
    H = D // HG_HEAD_DIM
    T = min(T, S)
    nT = S // T
    blk = (None, T, HG_HEAD_DIM)
    fwd = lambda off: pl.BlockSpec(blk, lambda b, h, t: (b, t, off + h))
    bwd = lambda off: pl.BlockSpec(blk, lambda b, h, t: (b, nT - 1 - t, off + h))
    out_shape = jax.ShapeDtypeStruct((B, S, D), F32)
    return pl.pallas_call(
        functools.partial(_hgrn_kernel, T=T),
        grid=(B, H, nT),
        in_specs=[fwd(0), fwd(H), fwd(3 * H), bwd(0), bwd(2 * H), bwd(3 * H),
                  pl.BlockSpec((2, HG_HEAD_DIM), lambda b, h, t: (0, h))],
        out_specs=[pl.BlockSpec(blk, lambda b, h, t: (b, t, h)),
                   pl.BlockSpec(blk, lambda b, h, t: (b, nT - 1 - t, h))],
        out_shape=[out_shape, out_shape],
        scratch_shapes=[pltpu.VMEM((HG_HEAD_DIM, HG_HEAD_DIM), F32),
                        pltpu.VMEM((HG_HEAD_DIM, HG_HEAD_DIM), F32)],
        compiler_params=_params(("parallel", "parallel", "arbitrary")),
        name="hgrn_recurrence",
    )(proj, proj, proj, proj, proj, proj, lb)


def _hgrn_out_kernel(of_ref, ob_ref, gate_ref, x_ref, wn_ref, wo_ref, out_ref):
    o = of_ref[...] + ob_ref[...]
    D = o.shape[-1]
    parts = []
    for h in range(D // HG_HEAD_DIM):
        oh = o[:, h * HG_HEAD_DIM:(h + 1) * HG_HEAD_DIM]
        parts.append(oh * lax.rsqrt(jnp.mean(oh * oh, axis=-1, keepdims=True) + EPS))
    gate = gate_ref[...].astype(F32)
    y = (jnp.concatenate(parts, axis=-1) * wn_ref[...]) * (gate * _sigmoid(gate))
    out_ref[...] = x_ref[...] + _dot(y.astype(BF16), wo_ref[...])


def _hgrn_out(o_f, o_b, proj, x, wn_tiled, w_out, *, tm=512):
    B, S, D = x.shape
    tm = min(tm, S)
    row = pl.BlockSpec((None, tm, D), lambda b, i: (b, i, 0))
    return pl.pallas_call(
        _hgrn_out_kernel,
        grid=(B, S // tm),
        in_specs=[row, row,
                  pl.BlockSpec((None, tm, D), lambda b, i: (b, i, 4)),
                  row,
                  pl.BlockSpec((1, D), lambda b, i: (0, 0)),
                  pl.BlockSpec((D, D), lambda b, i: (0, 0))],
        out_specs=row,
        out_shape=jax.ShapeDtypeStruct((B, S, D), F32),
        compiler_params=_params(("parallel", "parallel")),
        name="hgrn_out",
    )(o_f, o_b, proj, x, wn_tiled, w_out)


def _swiglu_step(h_ref, wg_ref, wu_ref, wd_ref, acc_ref):
    h = h_ref[...]
    g = _dot(h, wg_ref[...])
    u = _dot(h, wu_ref[...])
    a = (g * _sigmoid(g)) * u
    acc_ref[...] += _dot(a.astype(BF16), wd_ref[...])


def _ffn_kernel(x_ref, nw_ref, wg_ref, wu_ref, wd_ref, o_ref, h_ref, acc_ref):
    j = pl.program_id(2)

    @pl.when(j == 0)
    def _():
        h_ref[...] = _rms(x_ref[...], nw_ref[...]).astype(BF16)
        acc_ref[...] = jnp.zeros_like(acc_ref)

    _swiglu_step(h_ref, wg_ref, wu_ref, wd_ref, acc_ref)

    @pl.when(j == pl.num_programs(2) - 1)
    def _():
        o_ref[...] = x_ref[...] + acc_ref[...]


def _ffn(x, nw, w_gate, w_up, w_down, *, tm=512, tf=1792):
    B, S, D = x.shape
    F = w_gate.shape[-1]
    tm = min(tm, S)
    row = pl.BlockSpec((None, tm, D), lambda b, i, j: (b, i, 0))
    return pl.pallas_call(
        _ffn_kernel,
        grid=(B, S // tm, F // tf),
        in_specs=[row,
                  pl.BlockSpec((1, D), lambda b, i, j: (0, 0)),
                  pl.BlockSpec((D, tf), lambda b, i, j: (0, j)),
                  pl.BlockSpec((D, tf), lambda b, i, j: (0, j)),
                  pl.BlockSpec((tf, D), lambda b, i, j: (j, 0))],
        out_specs=row,
        out_shape=jax.ShapeDtypeStruct((B, S, D), F32),
        scratch_shapes=[pltpu.VMEM((tm, D), BF16), pltpu.VMEM((tm, D), F32)],
        compiler_params=_params(("parallel", "parallel", "arbitrary")),
        name="swiglu_ffn",
    )(x, nw, w_gate, w_up, w_down)


MOE_RB = 512
MOE_TF = 1792
ROUTER_TM = 512
DISPATCH_TM = 512
COMBINE_TM = 256


def _router_kernel(x_ref, nw_ref, whi_ref, wlo_ref, tri_ref, w_ref, i_ref, cnt_ref, run_ref):
    @pl.when(pl.program_id(0) == 0)
    def _():
        run_ref[...] = jnp.zeros_like(run_ref)

    h = _rms(x_ref[...], nw_ref[...])
    h_hi = h.astype(BF16)
    h_lo = (h - h_hi.astype(F32)).astype(BF16)
    whi = whi_ref[...]
    logits = _dot(h_hi, whi) + _dot(h_hi, wlo_ref[...]) + _dot(h_lo, whi)

    tm = h.shape[0]
    lane = lax.broadcasted_iota(jnp.int32, (tm, LANES), 1)
    lane_f = lane.astype(F32)

    def top(vals):
        m = jnp.max(vals, axis=-1, keepdims=True)
        idx = jnp.min(jnp.where(vals == m, lane_f, float(LANES)), axis=-1, keepdims=True)
        return m, idx.astype(jnp.int32)

    logits = jnp.where(lane < N_EXPERTS, logits, -jnp.inf)
    m1, i1 = top(logits)
    m2, i2 = top(jnp.where(lane == i1, -jnp.inf, logits))
    e2 = jnp.exp(m2 - m1)
    w1 = 1.0 / (1.0 + e2)
    w2 = e2 / (1.0 + e2)

    oh1 = jnp.where(lane == i1, 1.0, 0.0)
    oh2 = jnp.where(lane == i2, 1.0, 0.0)
    oh = oh1 + oh2
    before = _dot(tri_ref[...], oh.astype(BF16)) + run_ref[...]
    r1 = jnp.sum(oh1 * before, axis=-1, keepdims=True).astype(jnp.int32)
    r2 = jnp.sum(oh2 * before, axis=-1, keepdims=True).astype(jnp.int32)
    run_ref[...] += jnp.sum(oh, axis=0, keepdims=True)
    cnt_ref[...] = run_ref[...].astype(jnp.int32)
    w_ref[...] = jnp.where(lane == 0, w1, jnp.where(lane == 1, w2, 0.0))
    i_ref[...] = jnp.where(lane == 0, i1, jnp.where(lane == 1, i2,
                           jnp.where(lane == 2, r1, jnp.where(lane == 3, r2, 0))))


def _router(x2, nw, w_hi, w_lo, tri, *, tm=ROUTER_TM):
    N, D = x2.shape
    row = lambda w: pl.BlockSpec((tm, w), lambda i: (i, 0))
    return pl.pallas_call(
        _router_kernel,
        grid=(N // tm,),
        in_specs=[row(D),
                  pl.BlockSpec((1, D), lambda i: (0, 0)),
                  pl.BlockSpec((D, LANES), lambda i: (0, 0)),
                  pl.BlockSpec((D, LANES), lambda i: (0, 0)),
                  pl.BlockSpec((tm, tm), lambda i: (0, 0))],
        out_specs=[row(LANES), row(LANES), pl.BlockSpec((1, LANES), lambda i: (0, 0))],
        out_shape=[jax.ShapeDtypeStruct((N, LANES), F32),
                   jax.ShapeDtypeStruct((N, LANES), jnp.int32),
                   jax.ShapeDtypeStruct((1, LANES), jnp.int32)],
        scratch_shapes=[pltpu.VMEM((1, LANES), F32)],
        compiler_params=_params(("arbitrary",)),
        name="moe_router",
    )(x2, nw, w_hi, w_lo, tri)


def _row_copy(src_hbm, src_row, dst_ref, dst_row, sem):
    return pltpu.make_async_copy(src_hbm.at[pl.ds(src_row, 1)], dst_ref.at[pl.ds(dst_row, 1)], sem)


def _dispatch_kernel(zero_blk_ref, n_zero_ref, dst_ref, x_ref, xs_hbm, zero_ref, sem, zsem,
                     *, tm):
    @pl.when(pl.program_id(0) == 0)
    def _():
        zero_ref[...] = jnp.zeros_like(zero_ref)
        rb = zero_ref.shape[0]

        def fill(c, carry):
            row = pl.multiple_of(zero_blk_ref[c] * rb, rb)
            copy = pltpu.make_async_copy(zero_ref, xs_hbm.at[pl.ds(row, rb)], zsem)
            copy.start()
            copy.wait()
            return carry

        lax.fori_loop(0, n_zero_ref[0], fill, 0)

    def issue(t, carry):
        _row_copy(x_ref, t, xs_hbm, dst_ref[0, 2 * t], sem).start(priority=0)
        _row_copy(x_ref, t, xs_hbm, dst_ref[0, 2 * t + 1], sem).start(priority=1)
        return carry

    lax.fori_loop(0, tm, issue, 0, unroll=8)
    for _ in range(2):
        pltpu.make_async_copy(x_ref, xs_hbm.at[pl.ds(0, tm)], sem).wait()


def _dispatch(x2, dst, zero_blk, n_zero, n_rows, *, tm=DISPATCH_TM):
    N, D = x2.shape
    return pl.pallas_call(
        functools.partial(_dispatch_kernel, tm=tm),
        grid_spec=pltpu.PrefetchScalarGridSpec(
            num_scalar_prefetch=2,
            grid=(N // tm,),
            in_specs=[pl.BlockSpec((None, 1, 2 * tm), lambda i, ps, pn: (i, 0, 0),
                                   memory_space=pltpu.SMEM),
                      pl.BlockSpec((tm, D), lambda i, ps, pn: (i, 0))],
            out_specs=pl.BlockSpec(memory_space=pl.ANY),
            scratch_shapes=[pltpu.VMEM((MOE_RB, D), F32),
                            pltpu.SemaphoreType.DMA(()), pltpu.SemaphoreType.DMA(())]),
        out_shape=jax.ShapeDtypeStruct((n_rows, D), F32),
        compiler_params=_params(("arbitrary",)),
        name="moe_dispatch",
    )(zero_blk, n_zero, dst.reshape(N // tm, 1, 2 * tm), x2)


def _expert_kernel(be_ref, nb_ref, x_ref, nw_ref, wg_ref, wu_ref, wd_ref, o_ref, h_ref, acc_ref):
    del be_ref
    j = pl.program_id(1)
    used = pl.program_id(0) < nb_ref[0]

    @pl.when(jnp.logical_and(jnp.logical_not(used), j == 0))
    def _():
        o_ref[...] = jnp.zeros_like(o_ref)

    @pl.when(used)
    def _():
        @pl.when(j == 0)
        def _():
            h_ref[...] = _rms(x_ref[...], nw_ref[...]).astype(BF16)
            acc_ref[...] = jnp.zeros_like(acc_ref)

        _swiglu_step(h_ref, wg_ref, wu_ref, wd_ref, acc_ref)

        @pl.when(j == pl.num_programs(1) - 1)
        def _():
            o_ref[...] = acc_ref[...]


def _experts(xs, nw, blk_expert, n_blk, w_gate, w_up, w_down, *, rb=MOE_RB, tf=MOE_TF):
    R, D = xs.shape
    F = w_gate.shape[-1]
    nj = F // tf
    used = lambda r, nb: jnp.minimum(r, nb[0] - 1)
    jj = lambda r, j, nb: jnp.where(r < nb[0], j, nj - 1)
    row = pl.BlockSpec((rb, D), lambda r, j, be, nb: (used(r, nb), 0))
    w_in = pl.BlockSpec((None, D, tf), lambda r, j, be, nb: (be[used(r, nb)], 0, jj(r, j, nb)))
    w_out = pl.BlockSpec((None, tf, D), lambda r, j, be, nb: (be[used(r, nb)], jj(r, j, nb), 0))
    return pl.pallas_call(
        _expert_kernel,
        grid_spec=pltpu.PrefetchScalarGridSpec(
            num_scalar_prefetch=2,
            grid=(R // rb, nj),
            in_specs=[row, pl.BlockSpec((1, D), lambda r, j, be, nb: (0, 0)), w_in, w_in, w_out],
            out_specs=pl.BlockSpec((rb, D), lambda r, j, be, nb: (r, 0)),
            scratch_shapes=[pltpu.VMEM((rb, D), BF16), pltpu.VMEM((rb, D), F32)]),
        out_shape=jax.ShapeDtypeStruct((R, D), F32),
        compiler_params=_params(("arbitrary", "arbitrary")),
        name="moe_experts",
    )(blk_expert, n_blk, xs, nw, w_gate, w_up, w_down)


def _combine_kernel(*refs, tm, final_norm):
    if final_norm:
        cur_ref, nxt_ref, w_ref, x_ref, y_hbm, fnw_ref, o_ref, ybuf, sem = refs
    else:
        cur_ref, nxt_ref, w_ref, x_ref, y_hbm, o_ref, ybuf, sem = refs
    i = pl.program_id(0)
    slot = i % 2

    def issue(idx_ref, s):
        def body(t, carry):
            _row_copy(y_hbm, idx_ref[0, 2 * t], ybuf.at[s], t, sem.at[s]).start(priority=0)
            _row_copy(y_hbm, idx_ref[0, 2 * t + 1], ybuf.at[s], tm + t,
                      sem.at[s]).start(priority=1)
            return carry
        lax.fori_loop(0, tm, body, 0, unroll=8)

    @pl.when(i == 0)
    def _():
        issue(cur_ref, 0)

    @pl.when(i + 1 < pl.num_programs(0))
    def _():
        issue(nxt_ref, 1 - slot)

    pltpu.make_async_copy(y_hbm.at[pl.ds(0, 2 * tm)], ybuf.at[slot], sem.at[slot]).wait()
    w = w_ref[...]
    y = ybuf[slot]
    out = x_ref[...] + w[:, 0:1] * y[0:tm] + w[:, 1:2] * y[tm:2 * tm]
    if final_norm:
        out = _rms(out, fnw_ref[...])
    o_ref[...] = out


def _combine(x2, y, dst, wts, final_nw, *, tm=COMBINE_TM):
    N, D = x2.shape
    nt = N // tm
    idx = dst.reshape(nt, 1, 2 * tm)
    smem = lambda f: pl.BlockSpec((None, 1, 2 * tm), f, memory_space=pltpu.SMEM)
    in_specs = [smem(lambda i: (i, 0, 0)),
                smem(lambda i: (jnp.minimum(i + 1, nt - 1), 0, 0)),
                pl.BlockSpec((tm, LANES), lambda i: (i, 0)),
                pl.BlockSpec((tm, D), lambda i: (i, 0)),
                pl.BlockSpec(memory_space=pl.ANY)]
    args = [idx, idx, wts, x2, y]
    if final_nw is not None:
        in_specs.append(pl.BlockSpec((1, D), lambda i: (0, 0)))
        args.append(final_nw)
    return pl.pallas_call(
        functools.partial(_combine_kernel, tm=tm, final_norm=final_nw is not None),
        grid=(nt,),
        in_specs=in_specs,
        out_specs=pl.BlockSpec((tm, D), lambda i: (i, 0)),
        out_shape=jax.ShapeDtypeStruct((N, D), F32),
        scratch_shapes=[pltpu.VMEM((2, 2 * tm, D), F32), pltpu.SemaphoreType.DMA((2,))],
        compiler_params=_params(("arbitrary",)),
        name="moe_combine",
    )(*args)


def _moe(x, nw, w_router_hi, w_router_lo, tri, w_gate, w_up, w_down, final_nw):
    B, S, D = x.shape
    N = B * S
    rb = MOE_RB
    x2 = x.reshape(N, D)
    wts, meta, cnt = _router(x2, nw, w_router_hi, w_router_lo, tri)
    counts = cnt[0, :N_EXPERTS]
    group = ((counts + rb - 1) // rb) * rb
    ends = jnp.cumsum(group)
    offs = ends - group
    dst = jnp.stack([offs[meta[:, 0]] + meta[:, 2], offs[meta[:, 1]] + meta[:, 3]], axis=-1)
    n_blocks = (2 * N) // rb + N_EXPERTS
    blk_expert = jnp.minimum(
        jnp.searchsorted(ends // rb, jnp.arange(n_blocks, dtype=jnp.int32), side="right"),
        N_EXPERTS - 1).astype(jnp.int32)
    n_blk = (ends[-1:] // rb).astype(jnp.int32)
    cand = jnp.concatenate([ends // rb - 1, n_blk + jnp.arange(N_EXPERTS, dtype=jnp.int32)])
    keep = jnp.concatenate([group > counts, n_blk + jnp.arange(N_EXPERTS) < n_blocks])
    zero_blk = cand[jnp.argsort(jnp.logical_not(keep), stable=True)].astype(jnp.int32)
    n_zero = jnp.sum(keep, keepdims=True).astype(jnp.int32)
    xs = _dispatch(x2, dst, zero_blk, n_zero, n_blocks * rb)
    y = _experts(xs, nw, blk_expert, n_blk, w_gate, w_up, w_down)
    return _combine(x2, y, dst, wts, final_nw).reshape(B, S, D)


def _attn_kernel(q_ref, k_ref, v_ref, kp_ref, kn_ref, vp_ref, vn_ref, bias_ref, o_ref, lse_ref,
                 kw_ref, vw_ref, *, L, TQ):
    R, QB, KW = ATT_RADIUS, ATT_QB, ATT_KW
    n = pl.program_id(2)
    kw_ref[0:R, :] = kp_ref[...]
    kw_ref[R:R + TQ, :] = k_ref[...]
    kw_ref[R + TQ:, :] = kn_ref[...]
    vw_ref[0:R, :] = vp_ref[...]
    vw_ref[R:R + TQ, :] = v_ref[...]
    vw_ref[R + TQ:, :] = vn_ref[...]

    kcol = lax.broadcasted_iota(jnp.int32, (QB, KW), 1)
    lane = lax.broadcasted_iota(jnp.int32, (QB, LANES), 1)
    lo_half = lane < ATT_HEAD_DIM

    def sub_block(i, at_edge):
        r0 = pl.multiple_of(i * QB, QB)
        if at_edge:
            kpos = n * TQ + i * QB - R + kcol
            in_seq = (kpos >= 0) & (kpos < L)
        lse_tile = jnp.zeros((QB, LANES), F32)
        for p in range(ATT_HEADS // 2):
            cs = slice(p * LANES, (p + 1) * LANES)
            q2 = q_ref[pl.ds(r0, QB), cs]
            k2 = kw_ref[pl.ds(r0, KW), cs]
            v2 = vw_ref[pl.ds(r0, KW), cs]
            outs = []
            for hh in range(2):
                h = 2 * p + hh
                qm = jnp.where(lo_half if hh == 0 else jnp.logical_not(lo_half), q2, 0.0)
                s = _dot_nt(qm.astype(BF16), k2) + bias_ref[h]
                if at_edge:
                    s = jnp.where(in_seq, s, -jnp.inf)
                m = jnp.max(s, axis=-1, keepdims=True)
                pe = jnp.exp2(s - m)
                den = jnp.sum(pe, axis=-1, keepdims=True)
                outs.append(_dot(pe.astype(BF16), v2) / den)
                lse_tile = jnp.where(lane == h, LN2 * (m + jnp.log2(den)), lse_tile)
            o_ref[pl.ds(r0, QB), cs] = jnp.where(lo_half, outs[0], outs[1])
        lse_ref[pl.ds(r0, QB), :] = lse_tile

    def sub(i, carry):
        q0 = n * TQ + i * QB
        at_edge = jnp.logical_or(q0 == 0, q0 + QB == L)
        pl.when(at_edge)(lambda: sub_block(i, True))
        pl.when(jnp.logical_not(at_edge))(lambda: sub_block(i, False))
        return carry

    lax.fori_loop(0, TQ // QB, sub, 0)


def _attn_bias(dil, slopes):
    rel = np.arange(ATT_KW)[None, :] - ATT_RADIUS - np.arange(ATT_QB)[:, None]
    arel = np.abs(rel).astype(np.float32)
    bias = -np.asarray(slopes, np.float32)[:, None, None] * (np.float32(dil) * arel)[None]
    bias = np.where((arel <= ATT_RADIUS)[None], bias * np.float32(LOG2E), -np.inf)
    return jnp.asarray(bias, F32)


def _attn_group(qkv, dil, slopes, *, TQ=512):
    B, _, L, D3 = qkv.shape
    D = D3 // 3
    R = ATT_RADIUS
    TQ = min(TQ, L)
    nq = L // TQ
    hb = TQ // R
    last_hb = L // R - 1
    main = lambda c: pl.BlockSpec((None, None, TQ, D), lambda b, r, n: (b, r, n, c))
    prev = lambda c: pl.BlockSpec((None, None, R, D),
                                  lambda b, r, n: (b, r, jnp.maximum(n * hb - 1, 0), c))
    nxt = lambda c: pl.BlockSpec((None, None, R, D),
                                 lambda b, r, n: (b, r, jnp.minimum((n + 1) * hb, last_hb), c))
    kern = functools.partial(_attn_kernel, L=L, TQ=TQ)
    return pl.pallas_call(
        kern,
        grid=(B, dil, nq),
        in_specs=[main(0), main(1), main(2), prev(1), nxt(1), prev(2), nxt(2),
                  pl.BlockSpec((ATT_HEADS, ATT_QB, ATT_KW), lambda b, r, n: (0, 0, 0))],
        out_specs=[pl.BlockSpec((None, None, TQ, D), lambda b, r, n: (b, r, n, 0)),
                   pl.BlockSpec((None, None, TQ, LANES), lambda b, r, n: (b, r, n, 0))],
        out_shape=[jax.ShapeDtypeStruct((B, dil, L, D), F32),
                   jax.ShapeDtypeStruct((B, dil, L, LANES), F32)],
        scratch_shapes=[pltpu.VMEM((TQ + 2 * R, D), BF16), pltpu.VMEM((TQ + 2 * R, D), BF16)],
        compiler_params=_params(("parallel", "parallel", "parallel")),
        name="dilated_attn",
    )(qkv, qkv, qkv, qkv, qkv, qkv, qkv, _attn_bias(dil, slopes))


def _attn_merge_kernel(o0_ref, o1_ref, o2_ref, l0_ref, l1_ref, l2_ref, x_ref, ex_ref, wo_ref,
                       out_ref, s1_ref, s2_ref, t1_ref, t2_ref, *, tm):
    n_chunks = s1_ref.shape[0]

    def interleave(src_ref, lse_src_ref, dst_ref, lse_dst_ref, dil):
        rows = tm // dil
        for r in range(dil):
            lse_dst_ref[pl.ds(r, rows, stride=dil), :] = lse_src_ref[r]
            for c in range(n_chunks):
                dst_ref[c, pl.ds(r, rows, stride=dil), :] = src_ref[r, :, c * LANES:(c + 1) * LANES]
        return jnp.concatenate([dst_ref[c] for c in range(n_chunks)], axis=1)

    o1 = interleave(o1_ref, l1_ref, s1_ref, t1_ref, ATT_DILATIONS[1])
    o2 = interleave(o2_ref, l2_ref, s2_ref, t2_ref, ATT_DILATIONS[2])
    la, lb, lc = l0_ref[0], t1_ref[...], t2_ref[...]
    m = jnp.maximum(jnp.maximum(la, lb), lc)
    ea, eb, ec = jnp.exp(la - m), jnp.exp(lb - m), jnp.exp(lc - m)
    den = ea + eb + ec
    ex = ex_ref[...]

    def expand(w):
        return _dot(w.astype(BF16), ex)

    wa, wb = expand(ea / den), expand(eb / den)
    o = wa * o0_ref[0] + wb * o1 + (1.0 - wa - wb) * o2
    out_ref[...] = x_ref[...] + _dot(o.astype(BF16), wo_ref[...])


def _attn_merge(outs, lses, x, expand_mat, w_out, *, tm=512):
    B, S, D = x.shape
    tm = min(tm, S)
    d0, d1, d2 = ATT_DILATIONS
    grp = lambda d, w: pl.BlockSpec((None, d, tm // d, w), lambda b, i: (b, 0, i, 0))
    row = pl.BlockSpec((None, tm, D), lambda b, i: (b, i, 0))
    return pl.pallas_call(
        functools.partial(_attn_merge_kernel, tm=tm),
        grid=(B, S // tm),
        in_specs=[grp(d0, D), grp(d1, D), grp(d2, D), grp(d0, LANES), grp(d1, LANES),
                  grp(d2, LANES), row,
                  pl.BlockSpec((LANES, D), lambda b, i: (0, 0)),
                  pl.BlockSpec((D, D), lambda b, i: (0, 0))],
        out_specs=row,
        out_shape=jax.ShapeDtypeStruct((B, S, D), F32),
        scratch_shapes=[pltpu.VMEM((D // LANES, tm, LANES), F32),
                        pltpu.VMEM((D // LANES, tm, LANES), F32),
                        pltpu.VMEM((tm, LANES), F32), pltpu.VMEM((tm, LANES), F32)],
        compiler_params=_params(("parallel", "parallel")),
        name="attn_merge",
    )(*outs, *lses, x, expand_mat, w_out)


def _alibi_slopes(n):
    return tuple(float(s) for s in np.asarray(2.0 ** (-8.0 * (np.arange(n) + 1) / n), np.float32))


def _trunk(x, p):
    depth = p["norm_mix"].shape[0]
    D = x.shape[-1]
    slopes = _alibi_slopes(ATT_HEADS)
    for i in range(depth):
        a = i // 2
        nw_mix = p["norm_mix"][i][None, :]
        nw_ffn = p["norm_ffn"][i][None, :]
        if i % 2 == 0:
            proj = _norm_proj(x, nw_mix, p["hg_w_in"][a], tm=512)[:, 0]
            o_f, o_b = _hgrn_recurrence(proj, p["lbs"][i])
            x = _hgrn_out(o_f, o_b, proj, x, p["hg_norm_tiled"][a], p["hg_w_out"][a])
            x = _ffn(x, nw_ffn, p["ffn_w_gate"][a], p["ffn_w_up"][a], p["ffn_w_down"][a])
        else:
            outs, lses = [], []
            for g, dil in enumerate(ATT_DILATIONS):
                w_g = p["attn_w_in"][a][:, 3 * D * g:3 * D * (g + 1)]
                assert D == PROJ_TN
                qkv = _norm_proj(x, nw_mix, w_g, dil=dil, first_scale=ATT_Q_SCALE)
                o_g, lse_g = _attn_group(qkv, dil, slopes)
                outs.append(o_g)
                lses.append(lse_g)
            x = _attn_merge(outs, lses, x, p["expand_mat"], p["attn_w_out"][a])
            final_nw = p["norm_out"][None, :] if i == depth - 1 else None
            x = _moe(x, nw_ffn, p["moe_router_hi"][a], p["moe_router_lo"][a], p["rank_tri"],
                     p["moe_w_gate"][a], p["moe_w_up"][a], p["moe_w_down"][a], final_nw)
    return x


def kernel(x_prompt, x_sample, norm_mix, norm_ffn, norm_out, hg_w_in, hg_lower_bounds, hg_norm,
           hg_w_out, attn_w_in, attn_w_out, ffn_w_gate, ffn_w_up, ffn_w_down, moe_router,
           moe_w_gate, moe_w_up, moe_w_down):
    D = x_prompt.shape[-1]
    assert norm_mix.shape[0] % 2 == 0, "trunk ends on an attention/MoE layer"
    pr = jax.nn.softmax(hg_lower_bounds.astype(F32), axis=0)
    head_of_lane = np.arange(D) // ATT_HEAD_DIM
    expand_mat = jnp.asarray(np.arange(LANES)[:, None] == head_of_lane[None, :], BF16)
    router = jnp.pad(moe_router.astype(F32), ((0, 0), (0, 0), (0, LANES - N_EXPERTS)))
    router_hi = router.astype(BF16)
    p = dict(
        norm_mix=norm_mix, norm_ffn=norm_ffn, norm_out=norm_out,
        lbs=jnp.cumsum(pr, axis=0) - pr[0],
        hg_w_in=hg_w_in.astype(BF16), hg_w_out=hg_w_out.astype(BF16),
        hg_norm_tiled=jnp.tile(hg_norm.astype(F32), (1, D // HG_HEAD_DIM))[:, None, :],
        attn_w_in=attn_w_in.astype(BF16), attn_w_out=attn_w_out.astype(BF16),
        ffn_w_gate=ffn_w_gate.astype(BF16), ffn_w_up=ffn_w_up.astype(BF16),
        ffn_w_down=ffn_w_down.astype(BF16),
        moe_router_hi=router_hi,
        moe_router_lo=(router - router_hi.astype(F32)).astype(BF16),
        moe_w_gate=moe_w_gate.astype(BF16), moe_w_up=moe_w_up.astype(BF16),
        moe_w_down=moe_w_down.astype(BF16),
        expand_mat=expand_mat,
        rank_tri=jnp.asarray(np.tri(ROUTER_TM, k=-1), BF16),
    )
    return (_trunk(x_prompt, p), _trunk(x_sample, p))
```

```python
import functools

import jax
import jax.numpy as jnp
import numpy as np
from jax import lax
from jax.experimental import pallas as pl
from jax.experimental.pallas import tpu as pltpu

F32 = jnp.float32
BF16 = jnp.bfloat16

EPS = 1e-6
HG_HEAD_DIM = 128
HG_CHUNK = 64
HG_SUB = 32
HG_EXP_CLAMP = 80.0
HG_UNROLL = 8
ATT_HEADS = 16
ATT_HEAD_DIM = 64
ATT_RADIUS = 64
ATT_DILATIONS = (1, 4, 16)
ATT_QB = 128
ATT_KW = ATT_QB + 2 * ATT_RADIUS
LOG2E = 1.4426950408889634
LN2 = 0.6931471805599453
ATT_Q_SCALE = ATT_HEAD_DIM ** -0.5 * LOG2E
N_EXPERTS = 8
LANES = 128
VMEM_LIMIT = 56 * 1024 * 1024


def _params(sem):
    return pltpu.CompilerParams(dimension_semantics=sem, vmem_limit_bytes=VMEM_LIMIT)


def _rms(x, w):
    return (x * lax.rsqrt(jnp.mean(x * x, axis=-1, keepdims=True) + EPS)) * w


def _sigmoid(x):
    return 1.0 / (1.0 + jnp.exp(-x))


def _dot(a, b):
    return jnp.dot(a, b, preferred_element_type=F32)


def _dot_nt(a, b):
    return lax.dot_general(a, b, (((1,), (1,)), ((), ())), preferred_element_type=F32)


def _dot_tn(a, b):
    return lax.dot_general(a, b, (((0,), (0,)), ((), ())), preferred_element_type=F32)


PROJ_TN = 1024


def _norm_proj_kernel(x_ref, nw_ref, w_ref, o_ref, h_ref, *xs_ref, dil, tm, first_scale):
    rows = tm // dil
    nw = nw_ref[...]
    if dil == 1:
        h_ref[...] = _rms(x_ref[...], nw).astype(BF16)
    else:
        xs = xs_ref[0]
        n_chunks = xs.shape[0]
        for c in range(n_chunks):
            xs[c] = x_ref[:, c * LANES:(c + 1) * LANES]
        for r in range(dil):
            xr = jnp.concatenate(
                [xs[c, pl.ds(r, rows, stride=dil), :] for c in range(n_chunks)], axis=1)
            h_ref[r * rows:(r + 1) * rows, :] = _rms(xr, nw).astype(BF16)

    h = h_ref[...]
    for j in range(w_ref.shape[1] // PROJ_TN):
        cols = slice(j * PROJ_TN, (j + 1) * PROJ_TN)
        acc = _dot(h, w_ref[:, cols])
        if j == 0 and first_scale is not None:
            acc = acc * first_scale
        for r in range(dil):
            o_ref[r, :, cols] = acc[r * rows:(r + 1) * rows].astype(o_ref.dtype)


def _norm_proj(x, nw, w, *, dil=1, tm=1024, first_scale=None, out_dtype=BF16):
    B, S, D = x.shape
    N = w.shape[1]
    tm = min(tm, S)
    kern = functools.partial(_norm_proj_kernel, dil=dil, tm=tm, first_scale=first_scale)
    return pl.pallas_call(
        kern,
        grid=(B, S // tm),
        in_specs=[
            pl.BlockSpec((None, tm, D), lambda b, i: (b, i, 0)),
            pl.BlockSpec((1, D), lambda b, i: (0, 0)),
            pl.BlockSpec((D, N), lambda b, i: (0, 0)),
        ],
        out_specs=pl.BlockSpec((None, dil, tm // dil, N), lambda b, i: (b, 0, i, 0)),
        out_shape=jax.ShapeDtypeStruct((B, dil, S // dil, N), out_dtype),
        scratch_shapes=[pltpu.VMEM((tm, D), BF16)]
        + ([pltpu.VMEM((D // LANES, tm, LANES), F32)] if dil > 1 else []),
        compiler_params=_params(("parallel", "parallel")),
        name="norm_proj",
    )(x, nw, w)


def _hgrn_gates(q, f_raw, lb, tri, rev):
    q = q.astype(F32)
    q = q * _sigmoid(q)
    f = lb + (1.0 - lb) * _sigmoid(f_raw.astype(F32))
    g = jnp.log(f)
    g1 = g.astype(BF16)
    d1 = g - g1.astype(F32)
    g2 = d1.astype(BF16)
    g3 = (d1 - g2.astype(F32)).astype(BF16)
    b = _dot(tri, g1) + _dot(tri, g2) + _dot(tri, g3)
    return q, 1.0 - f, b


def _hgrn_scores(q, k, v, b, rev):
    C, SUB = HG_CHUNK, HG_SUB
    b_tot = b[0:1, :] if rev else b[C - 1:C, :]
    q_inter = (q * jnp.exp(b)).astype(BF16)
    kk = (k * jnp.exp(b_tot - b)).astype(BF16)
    st_inc = _dot_tn(v, kk)
    scores = []
    for i in range(C // SUB):
        lo, hi = i * SUB, (i + 1) * SUB
        if rev:
            klo, khi, mid = lo, C, lo + SUB // 2
        else:
            klo, khi, mid = 0, hi, lo + SUB // 2 - 1
        mu = b[mid:mid + 1, :]
        qi = (q[lo:hi] * jnp.exp(jnp.minimum(b[lo:hi] - mu, HG_EXP_CLAMP))).astype(BF16)
        ki = (k[klo:khi] * jnp.exp(jnp.minimum(mu - b[klo:khi], HG_EXP_CLAMP))).astype(BF16)
        scores.append(_dot_nt(qi, ki))
    return q_inter, jnp.exp(b_tot), st_inc, scores


def _hgrn_outputs(scores, v, q_inter, st, rev):
    C, SUB = HG_CHUNK, HG_SUB
    outs = []
    for i, a in enumerate(scores):
        lo = i * SUB
        klo, khi = (lo, C) if rev else (0, lo + SUB)
        t_idx = lo + lax.broadcasted_iota(jnp.int32, a.shape, 0)
        s_idx = klo + lax.broadcasted_iota(jnp.int32, a.shape, 1)
        a = jnp.where(s_idx >= t_idx if rev else s_idx <= t_idx, a, 0.0)
        outs.append(_dot(a.astype(BF16), v[klo:khi]))
    return _dot_nt(q_inter, st.astype(BF16)) + jnp.concatenate(outs, axis=0)


def _hgrn_kernel(qf_ref, ff_ref, vf_ref, qb_ref, fb_ref, vb_ref, lb_ref, of_ref, ob_ref,
                 sf_ref, sb_ref, *, T):
    C, U = HG_CHUNK, HG_UNROLL
    span = C * U

    @pl.when(pl.program_id(2) == 0)
    def _():
        sf_ref[...] = jnp.zeros_like(sf_ref)
        sb_ref[...] = jnp.zeros_like(sb_ref)

    row = lax.broadcasted_iota(jnp.int32, (C, C), 0)
    col = lax.broadcasted_iota(jnp.int32, (C, C), 1)
    tri_f = jnp.where(col <= row, 1.0, 0.0).astype(BF16)
    tri_b = jnp.where(col >= row, 1.0, 0.0).astype(BF16)
    lbf = lb_ref[0:1, :]
    lbb = lb_ref[1:2, :]
    n_spans = T // span

    def body(it, carry):
        r_f = pl.multiple_of(it * span, span)
        r_b = pl.multiple_of((n_spans - 1 - it) * span, span)
        rows_f, rows_b = pl.ds(r_f, span), pl.ds(r_b, span)
        qf, ff, vf = qf_ref[rows_f, :], ff_ref[rows_f, :], vf_ref[rows_f, :]
        qb, fb, vb = qb_ref[rows_b, :], fb_ref[rows_b, :], vb_ref[rows_b, :]
        chains = []
        for u in range(U):
            cf = slice(u * C, (u + 1) * C)
            chains.append((qf[cf], ff[cf], vf[cf], lbf, tri_f, False))
            cb = slice((U - 1 - u) * C, (U - u) * C)
            chains.append((qb[cb], fb[cb], vb[cb], lbb, tri_b, True))
        gates = [_hgrn_gates(q, f, lb, tri, rev) for q, f, _, lb, tri, rev in chains]
        mids = [_hgrn_scores(q, k, ch[2], b, ch[5]) for (q, k, b), ch in zip(gates, chains)]
        st = {False: sf_ref[...], True: sb_ref[...]}
        outs = {False: [], True: []}
        for (q_inter, decay, st_inc, scores), ch in zip(mids, chains):
            rev = ch[5]
            outs[rev].append(_hgrn_outputs(scores, ch[2], q_inter, st[rev], rev))
            st[rev] = st[rev] * decay + st_inc
        of_ref[rows_f, :] = jnp.concatenate(outs[False], axis=0).astype(of_ref.dtype)
        ob_ref[rows_b, :] = jnp.concatenate(outs[True][::-1], axis=0).astype(ob_ref.dtype)
        sf_ref[...] = st[False]
        sb_ref[...] = st[True]
        return carry

    lax.fori_loop(0, n_spans, body, 0)


def _hgrn_recurrence(proj, lb, *, T=2048):
    B, S, D5 = proj.shape
    D = D5 // 5
    H = D // HG_HEAD_DIM
    T = min(T, S)
    nT = S // T
    blk = (None, T, HG_HEAD_DIM)
    fwd = lambda off: pl.BlockSpec(blk, lambda b, h, t: (b, t, off + h))
    bwd = lambda off: pl.BlockSpec(blk, lambda b, h, t: (b, nT - 1 - t, off + h))
    out_shape = jax.ShapeDtypeStruct((B, S, D), BF16)
    return pl.pallas_call(
        functools.partial(_hgrn_kernel, T=T),
        grid=(B, H, nT),
        in_specs=[fwd(0), fwd(H), fwd(3 * H), bwd(0), bwd(2 * H), bwd(3 * H),
                  pl.BlockSpec((2, HG_HEAD_DIM), lambda b, h, t: (0, h))],
        out_specs=[pl.BlockSpec(blk, lambda b, h, t: (b, t, h)),
                   pl.BlockSpec(blk, lambda b, h, t: (b, nT - 1 - t, h))],
        out_shape=[out_shape, out_shape],
        scratch_shapes=[pltpu.VMEM((HG_HEAD_DIM, HG_HEAD_DIM), F32),
                        pltpu.VMEM((HG_HEAD_DIM, HG_HEAD_DIM), F32)],
        compiler_params=_params(("parallel", "parallel", "arbitrary")),
        name="hgrn_recurrence",
    )(proj, proj, proj, proj, proj, proj, lb)


def _hgrn_out_kernel(of_ref, ob_ref, gate_ref, x_ref, wn_ref, wo_ref, out_ref):
    o = of_ref[...].astype(F32) + ob_ref[...].astype(F32)
    D = o.shape[-1]
    parts = []
    for h in range(D // HG_HEAD_DIM):
        oh = o[:, h * HG_HEAD_DIM:(h + 1) * HG_HEAD_DIM]
        parts.append(oh * lax.rsqrt(jnp.mean(oh * oh, axis=-1, keepdims=True) + EPS))
    gate = gate_ref[...].astype(F32)
    y = (jnp.concatenate(parts, axis=-1) * wn_ref[...]) * (gate * _sigmoid(gate))
    out_ref[...] = x_ref[...] + _dot(y.astype(BF16), wo_ref[...])


def _hgrn_out(o_f, o_b, proj, x, wn_tiled, w_out, *, tm=512):
    B, S, D = x.shape
    tm = min(tm, S)
    row = pl.BlockSpec((None, tm, D), lambda b, i: (b, i, 0))
    return pl.pallas_call(
        _hgrn_out_kernel,
        grid=(B, S // tm),
        in_specs=[row, row,
                  pl.BlockSpec((None, tm, D), lambda b, i: (b, i, 4)),
                  row,
                  pl.BlockSpec((1, D), lambda b, i: (0, 0)),
                  pl.BlockSpec((D, D), lambda b, i: (0, 0))],
        out_specs=row,
        out_shape=jax.ShapeDtypeStruct((B, S, D), F32),
        compiler_params=_params(("parallel", "parallel")),
        name="hgrn_out",
    )(o_f, o_b, proj, x, wn_tiled, w_out)


def _swiglu_step(h_ref, wg_ref, wu_ref, wd_ref, acc_ref):
    h = h_ref[...]
    g = _dot(h, wg_ref[...])
    u = _dot(h, wu_ref[...])
    a = (g * _sigmoid(g)) * u
    acc_ref[...] += _dot(a.astype(BF16), wd_ref[...])


def _ffn_kernel(x_ref, nw_ref, wg_ref, wu_ref, wd_ref, o_ref, h_ref, acc_ref):
    j = pl.program_id(2)

    @pl.when(j == 0)
    def _():
        h_ref[...] = _rms(x_ref[...], nw_ref[...]).astype(BF16)
        acc_ref[...] = jnp.zeros_like(acc_ref)

    _swiglu_step(h_ref, wg_ref, wu_ref, wd_ref, acc_ref)

    @pl.when(j == pl.num_programs(2) - 1)
    def _():
        o_ref[...] = x_ref[...] + acc_ref[...]


def _ffn(x, nw, w_gate, w_up, w_down, *, tm=512, tf=1792):
    B, S, D = x.shape
    F = w_gate.shape[-1]
    tm = min(tm, S)
    row = pl.BlockSpec((None, tm, D), lambda b, i, j: (b, i, 0))
    return pl.pallas_call(
        _ffn_kernel,
        grid=(B, S // tm, F // tf),
        in_specs=[row,
                  pl.BlockSpec((1, D), lambda b, i, j: (0, 0)),
                  pl.BlockSpec((D, tf), lambda b, i, j: (0, j)),
                  pl.BlockSpec((D, tf), lambda b, i, j: (0, j)),
                  pl.BlockSpec((tf, D), lambda b, i, j: (j, 0))],
        out_specs=row,
        out_shape=jax.ShapeDtypeStruct((B, S, D), F32),
        scratch_shapes=[pltpu.VMEM((tm, D), BF16), pltpu.VMEM((tm, D), F32)],
        compiler_params=_params(("parallel", "parallel", "arbitrary")),
        name="swiglu_ffn",
    )(x, nw, w_gate, w_up, w_down)


MOE_RB = 512
MOE_TF = 1792
ROUTER_TM = 512
DISPATCH_TM = 512
COMBINE_TM = 256


def _router_kernel(x_ref, nw_ref, whi_ref, wlo_ref, tri_ref, w_ref, i_ref, cnt_ref, run_ref):
    @pl.when(pl.program_id(0) == 0)
    def _():
        run_ref[...] = jnp.zeros_like(run_ref)

    h = _rms(x_ref[...], nw_ref[...])
    h_hi = h.astype(BF16)
    h_lo = (h - h_hi.astype(F32)).astype(BF16)
    whi = whi_ref[...]
    logits = _dot(h_hi, whi) + _dot(h_hi, wlo_ref[...]) + _dot(h_lo, whi)

    tm = h.shape[0]
    lane = lax.broadcasted_iota(jnp.int32, (tm, LANES), 1)
    lane_f = lane.astype(F32)

    def top(vals):
        m = jnp.max(vals, axis=-1, keepdims=True)
        idx = jnp.min(jnp.where(vals == m, lane_f, float(LANES)), axis=-1, keepdims=True)
        return m, idx.astype(jnp.int32)

    logits = jnp.where(lane < N_EXPERTS, logits, -jnp.inf)
    m1, i1 = top(logits)
    m2, i2 = top(jnp.where(lane == i1, -jnp.inf, logits))
    e2 = jnp.exp(m2 - m1)
    w1 = 1.0 / (1.0 + e2)
    w2 = e2 / (1.0 + e2)

    oh1 = jnp.where(lane == i1, 1.0, 0.0)
    oh2 = jnp.where(lane == i2, 1.0, 0.0)
    oh = oh1 + oh2
    before = _dot(tri_ref[...], oh.astype(BF16)) + run_ref[...]
    r1 = jnp.sum(oh1 * before, axis=-1, keepdims=True).astype(jnp.int32)
    r2 = jnp.sum(oh2 * before, axis=-1, keepdims=True).astype(jnp.int32)
    run_ref[...] += jnp.sum(oh, axis=0, keepdims=True)
    cnt_ref[...] = run_ref[...].astype(jnp.int32)
    w_ref[...] = jnp.where(lane == 0, w1, jnp.where(lane == 1, w2, 0.0))
    i_ref[...] = jnp.where(lane == 0, i1, jnp.where(lane == 1, i2,
                           jnp.where(lane == 2, r1, jnp.where(lane == 3, r2, 0))))


def _router(x2, nw, w_hi, w_lo, tri, *, tm=ROUTER_TM):
    N, D = x2.shape
    row = lambda w: pl.BlockSpec((tm, w), lambda i: (i, 0))
    return pl.pallas_call(
        _router_kernel,
        grid=(N // tm,),
        in_specs=[row(D),
                  pl.BlockSpec((1, D), lambda i: (0, 0)),
                  pl.BlockSpec((D, LANES), lambda i: (0, 0)),
                  pl.BlockSpec((D, LANES), lambda i: (0, 0)),
                  pl.BlockSpec((tm, tm), lambda i: (0, 0))],
        out_specs=[row(LANES), row(LANES), pl.BlockSpec((1, LANES), lambda i: (0, 0))],
        out_shape=[jax.ShapeDtypeStruct((N, LANES), F32),
                   jax.ShapeDtypeStruct((N, LANES), jnp.int32),
                   jax.ShapeDtypeStruct((1, LANES), jnp.int32)],
        scratch_shapes=[pltpu.VMEM((1, LANES), F32)],
        compiler_params=_params(("arbitrary",)),
        name="moe_router",
    )(x2, nw, w_hi, w_lo, tri)


def _row_copy(src_hbm, src_row, dst_ref, dst_row, sem):
    return pltpu.make_async_copy(src_hbm.at[pl.ds(src_row, 1)], dst_ref.at[pl.ds(dst_row, 1)], sem)


def _dispatch_kernel(zero_blk_ref, n_zero_ref, dst_ref, x_ref, xs_hbm, zero_ref, sem, zsem,
                     *, tm):
    @pl.when(pl.program_id(0) == 0)
    def _():
        zero_ref[...] = jnp.zeros_like(zero_ref)
        rb = zero_ref.shape[0]

        def fill(c, carry):
            row = pl.multiple_of(zero_blk_ref[c] * rb, rb)
            copy = pltpu.make_async_copy(zero_ref, xs_hbm.at[pl.ds(row, rb)], zsem)
            copy.start()
            copy.wait()
            return carry

        lax.fori_loop(0, n_zero_ref[0], fill, 0)

    def issue(t, carry):
        _row_copy(x_ref, t, xs_hbm, dst_ref[0, 2 * t], sem).start()
        _row_copy(x_ref, t, xs_hbm, dst_ref[0, 2 * t + 1], sem).start()
        return carry

    lax.fori_loop(0, tm, issue, 0, unroll=8)
    for _ in range(2):
        pltpu.make_async_copy(x_ref, xs_hbm.at[pl.ds(0, tm)], sem).wait()


def _dispatch(x2, dst, zero_blk, n_zero, n_rows, *, tm=DISPATCH_TM):
    N, D = x2.shape
    return pl.pallas_call(
        functools.partial(_dispatch_kernel, tm=tm),
        grid_spec=pltpu.PrefetchScalarGridSpec(
            num_scalar_prefetch=2,
            grid=(N // tm,),
            in_specs=[pl.BlockSpec((None, 1, 2 * tm), lambda i, ps, pn: (i, 0, 0),
                                   memory_space=pltpu.SMEM),
                      pl.BlockSpec((tm, D), lambda i, ps, pn: (i, 0))],
            out_specs=pl.BlockSpec(memory_space=pl.ANY),
            scratch_shapes=[pltpu.VMEM((MOE_RB, D), F32),
                            pltpu.SemaphoreType.DMA(()), pltpu.SemaphoreType.DMA(())]),
        out_shape=jax.ShapeDtypeStruct((n_rows, D), F32),
        compiler_params=_params(("arbitrary",)),
        name="moe_dispatch",
    )(zero_blk, n_zero, dst.reshape(N // tm, 1, 2 * tm), x2)


def _expert_kernel(be_ref, nb_ref, x_ref, nw_ref, wg_ref, wu_ref, wd_ref, o_ref, h_ref, acc_ref):
    del be_ref
    j = pl.program_id(1)
    used = pl.program_id(0) < nb_ref[0]

    @pl.when(jnp.logical_and(jnp.logical_not(used), j == 0))
    def _():
        o_ref[...] = jnp.zeros_like(o_ref)

    @pl.when(used)
    def _():
        @pl.when(j == 0)
        def _():
            h_ref[...] = _rms(x_ref[...], nw_ref[...]).astype(BF16)
            acc_ref[...] = jnp.zeros_like(acc_ref)

        _swiglu_step(h_ref, wg_ref, wu_ref, wd_ref, acc_ref)

        @pl.when(j == pl.num_programs(1) - 1)
        def _():
            o_ref[...] = acc_ref[...]


def _experts(xs, nw, blk_expert, n_blk, w_gate, w_up, w_down, *, rb=MOE_RB, tf=MOE_TF):
    R, D = xs.shape
    F = w_gate.shape[-1]
    nj = F // tf
    used = lambda r, nb: jnp.minimum(r, nb[0] - 1)
    jj = lambda r, j, nb: jnp.where(r < nb[0], j, nj - 1)
    row = pl.BlockSpec((rb, D), lambda r, j, be, nb: (used(r, nb), 0))
    w_in = pl.BlockSpec((None, D, tf), lambda r, j, be, nb: (be[used(r, nb)], 0, jj(r, j, nb)))
    w_out = pl.BlockSpec((None, tf, D), lambda r, j, be, nb: (be[used(r, nb)], jj(r, j, nb), 0))
    return pl.pallas_call(
        _expert_kernel,
        grid_spec=pltpu.PrefetchScalarGridSpec(
            num_scalar_prefetch=2,
            grid=(R // rb, nj),
            in_specs=[row, pl.BlockSpec((1, D), lambda r, j, be, nb: (0, 0)), w_in, w_in, w_out],
            out_specs=pl.BlockSpec((rb, D), lambda r, j, be, nb: (r, 0)),
            scratch_shapes=[pltpu.VMEM((rb, D), BF16), pltpu.VMEM((rb, D), F32)]),
        out_shape=jax.ShapeDtypeStruct((R, D), F32),
        compiler_params=_params(("arbitrary", "arbitrary")),
        name="moe_experts",
    )(blk_expert, n_blk, xs, nw, w_gate, w_up, w_down)


def _combine_kernel(*refs, tm, final_norm):
    if final_norm:
        cur_ref, nxt_ref, w_ref, x_ref, y_hbm, fnw_ref, o_ref, ybuf, sem = refs
    else:
        cur_ref, nxt_ref, w_ref, x_ref, y_hbm, o_ref, ybuf, sem = refs
    i = pl.program_id(0)
    slot = i % 2

    def issue(idx_ref, s):
        def body(t, carry):
            _row_copy(y_hbm, idx_ref[0, 2 * t], ybuf.at[s], t, sem.at[s]).start()
            _row_copy(y_hbm, idx_ref[0, 2 * t + 1], ybuf.at[s], tm + t, sem.at[s]).start()
            return carry
        lax.fori_loop(0, tm, body, 0, unroll=8)

    @pl.when(i == 0)
    def _():
        issue(cur_ref, 0)

    @pl.when(i + 1 < pl.num_programs(0))
    def _():
        issue(nxt_ref, 1 - slot)

    pltpu.make_async_copy(y_hbm.at[pl.ds(0, 2 * tm)], ybuf.at[slot], sem.at[slot]).wait()
    w = w_ref[...]
    y = ybuf[slot]
    out = x_ref[...] + w[:, 0:1] * y[0:tm] + w[:, 1:2] * y[tm:2 * tm]
    if final_norm:
        out = _rms(out, fnw_ref[...])
    o_ref[...] = out


def _combine(x2, y, dst, wts, final_nw, *, tm=COMBINE_TM):
    N, D = x2.shape
    nt = N // tm
    idx = dst.reshape(nt, 1, 2 * tm)
    smem = lambda f: pl.BlockSpec((None, 1, 2 * tm), f, memory_space=pltpu.SMEM)
    in_specs = [smem(lambda i: (i, 0, 0)),
                smem(lambda i: (jnp.minimum(i + 1, nt - 1), 0, 0)),
                pl.BlockSpec((tm, LANES), lambda i: (i, 0)),
                pl.BlockSpec((tm, D), lambda i: (i, 0)),
                pl.BlockSpec(memory_space=pl.ANY)]
    args = [idx, idx, wts, x2, y]
    if final_nw is not None:
        in_specs.append(pl.BlockSpec((1, D), lambda i: (0, 0)))
        args.append(final_nw)
    return pl.pallas_call(
        functools.partial(_combine_kernel, tm=tm, final_norm=final_nw is not None),
        grid=(nt,),
        in_specs=in_specs,
        out_specs=pl.BlockSpec((tm, D), lambda i: (i, 0)),
        out_shape=jax.ShapeDtypeStruct((N, D), F32),
        scratch_shapes=[pltpu.VMEM((2, 2 * tm, D), F32), pltpu.SemaphoreType.DMA((2,))],
        compiler_params=_params(("arbitrary",)),
        name="moe_combine",
    )(*args)


def _moe(x, nw, w_router_hi, w_router_lo, tri, w_gate, w_up, w_down, final_nw):
    B, S, D = x.shape
    N = B * S
    rb = MOE_RB
    x2 = x.reshape(N, D)
    wts, meta, cnt = _router(x2, nw, w_router_hi, w_router_lo, tri)
    counts = cnt[0, :N_EXPERTS]
    group = ((counts + rb - 1) // rb) * rb
    ends = jnp.cumsum(group)
    offs = ends - group
    dst = jnp.stack([offs[meta[:, 0]] + meta[:, 2], offs[meta[:, 1]] + meta[:, 3]], axis=-1)
    n_blocks = (2 * N) // rb + N_EXPERTS
    blk_expert = jnp.minimum(
        jnp.searchsorted(ends // rb, jnp.arange(n_blocks, dtype=jnp.int32), side="right"),
        N_EXPERTS - 1).astype(jnp.int32)
    n_blk = (ends[-1:] // rb).astype(jnp.int32)
    cand = jnp.concatenate([ends // rb - 1, n_blk + jnp.arange(N_EXPERTS, dtype=jnp.int32)])
    keep = jnp.concatenate([group > counts, n_blk + jnp.arange(N_EXPERTS) < n_blocks])
    zero_blk = cand[jnp.argsort(jnp.logical_not(keep), stable=True)].astype(jnp.int32)
    n_zero = jnp.sum(keep, keepdims=True).astype(jnp.int32)
    xs = _dispatch(x2, dst, zero_blk, n_zero, n_blocks * rb)
    y = _experts(xs, nw, blk_expert, n_blk, w_gate, w_up, w_down)
    return _combine(x2, y, dst, wts, final_nw).reshape(B, S, D)


def _attn_kernel(q_ref, k_ref, v_ref, kp_ref, kn_ref, vp_ref, vn_ref, bias_ref, o_ref, lse_ref,
                 kw_ref, vw_ref, *, L, TQ):
    R, QB, KW = ATT_RADIUS, ATT_QB, ATT_KW
    n = pl.program_id(2)
    kw_ref[0:R, :] = kp_ref[...]
    kw_ref[R:R + TQ, :] = k_ref[...]
    kw_ref[R + TQ:, :] = kn_ref[...]
    vw_ref[0:R, :] = vp_ref[...]
    vw_ref[R:R + TQ, :] = v_ref[...]
    vw_ref[R + TQ:, :] = vn_ref[...]

    kcol = lax.broadcasted_iota(jnp.int32, (QB, KW), 1)
    lane = lax.broadcasted_iota(jnp.int32, (QB, LANES), 1)
    lo_half = lane < ATT_HEAD_DIM

    def sub_block(i, at_edge):
        r0 = pl.multiple_of(i * QB, QB)
        if at_edge:
            kpos = n * TQ + i * QB - R + kcol
            in_seq = (kpos >= 0) & (kpos < L)
        scores = []
        for h in range(ATT_HEADS):
            cs = slice(h // 2 * LANES, (h // 2 + 1) * LANES)
            q2 = q_ref[pl.ds(r0, QB), cs]
            qm = jnp.where(lo_half if h % 2 == 0 else jnp.logical_not(lo_half), q2, 0.0)
            scores.append(_dot_nt(qm.astype(BF16), kw_ref[pl.ds(r0, KW), cs]) + bias_ref[h])
        lse_tile = jnp.zeros((QB, LANES), F32)
        probs = []
        for h, s in enumerate(scores):
            if at_edge:
                s = jnp.where(in_seq, s, -jnp.inf)
            m = jnp.max(s, axis=-1, keepdims=True)
            pe = jnp.exp2(s - m)
            den = jnp.sum(pe, axis=-1, keepdims=True)
            probs.append((pe.astype(BF16), den))
            lse_tile = jnp.where(lane == h, LN2 * (m + jnp.log2(den)), lse_tile)
        lse_ref[pl.ds(r0, QB), :] = lse_tile
        outs = []
        for h, (pe, den) in enumerate(probs):
            cs = slice(h // 2 * LANES, (h // 2 + 1) * LANES)
            outs.append(_dot(pe, vw_ref[pl.ds(r0, KW), cs]) / den)
            if h % 2 == 1:
                o_ref[pl.ds(r0, QB), cs] = jnp.where(
                    lo_half, outs[h - 1], outs[h]).astype(o_ref.dtype)

    def sub(i, carry):
        q0 = n * TQ + i * QB
        at_edge = jnp.logical_or(q0 == 0, q0 + QB == L)
        pl.when(at_edge)(lambda: sub_block(i, True))
        pl.when(jnp.logical_not(at_edge))(lambda: sub_block(i, False))
        return carry

    lax.fori_loop(0, TQ // QB, sub, 0)


def _attn_bias(dil, slopes):
    rel = np.arange(ATT_KW)[None, :] - ATT_RADIUS - np.arange(ATT_QB)[:, None]
    arel = np.abs(rel).astype(np.float32)
    bias = -np.asarray(slopes, np.float32)[:, None, None] * (np.float32(dil) * arel)[None]
    bias = np.where((arel <= ATT_RADIUS)[None], bias * np.float32(LOG2E), -np.inf)
    return jnp.asarray(bias, F32)


def _attn_group(qkv, dil, slopes, *, TQ=512):
    B, _, L, D3 = qkv.shape
    D = D3 // 3
    R = ATT_RADIUS
    TQ = min(TQ, L)
    nq = L // TQ
    hb = TQ // R
    last_hb = L // R - 1
    main = lambda c: pl.BlockSpec((None, None, TQ, D), lambda b, r, n: (b, r, n, c))
    prev = lambda c: pl.BlockSpec((None, None, R, D),
                                  lambda b, r, n: (b, r, jnp.maximum(n * hb - 1, 0), c))
    nxt = lambda c: pl.BlockSpec((None, None, R, D),
                                 lambda b, r, n: (b, r, jnp.minimum((n + 1) * hb, last_hb), c))
    kern = functools.partial(_attn_kernel, L=L, TQ=TQ)
    return pl.pallas_call(
        kern,
        grid=(B, dil, nq),
        in_specs=[main(0), main(1), main(2), prev(1), nxt(1), prev(2), nxt(2),
                  pl.BlockSpec((ATT_HEADS, ATT_QB, ATT_KW), lambda b, r, n: (0, 0, 0))],
        out_specs=[pl.BlockSpec((None, None, TQ, D), lambda b, r, n: (b, r, n, 0)),
                   pl.BlockSpec((None, None, TQ, LANES), lambda b, r, n: (b, r, n, 0))],
        out_shape=[jax.ShapeDtypeStruct((B, dil, L, D), BF16),
                   jax.ShapeDtypeStruct((B, dil, L, LANES), F32)],
        scratch_shapes=[pltpu.VMEM((TQ + 2 * R, D), BF16), pltpu.VMEM((TQ + 2 * R, D), BF16)],
        compiler_params=_params(("parallel", "parallel", "parallel")),
        name="dilated_attn",
    )(qkv, qkv, qkv, qkv, qkv, qkv, qkv, _attn_bias(dil, slopes))


def _attn_merge_kernel(o0_ref, o1_ref, o2_ref, l0_ref, l1_ref, l2_ref, x_ref, ex_ref, wo_ref,
                       out_ref, s1_ref, s2_ref, t1_ref, t2_ref, *, tm):
    n_chunks = s1_ref.shape[0]

    def interleave(src_ref, lse_src_ref, dst_ref, lse_dst_ref, dil):
        rows = tm // dil
        for r in range(dil):
            lse_dst_ref[pl.ds(r, rows, stride=dil), :] = lse_src_ref[r]
            for c in range(n_chunks):
                dst_ref[c, pl.ds(r, rows, stride=dil), :] = (
                    src_ref[r, :, c * LANES:(c + 1) * LANES].astype(F32))
        return jnp.concatenate([dst_ref[c] for c in range(n_chunks)], axis=1)

    o1 = interleave(o1_ref, l1_ref, s1_ref, t1_ref, ATT_DILATIONS[1])
    o2 = interleave(o2_ref, l2_ref, s2_ref, t2_ref, ATT_DILATIONS[2])
    la, lb, lc = l0_ref[0], t1_ref[...], t2_ref[...]
    m = jnp.maximum(jnp.maximum(la, lb), lc)
    ea, eb, ec = jnp.exp(la - m), jnp.exp(lb - m), jnp.exp(lc - m)
    den = ea + eb + ec
    ex = ex_ref[...]

    def expand(w):
        return _dot(w.astype(BF16), ex)

    wa, wb = expand(ea / den), expand(eb / den)
    o = wa * o0_ref[0].astype(F32) + wb * o1 + (1.0 - wa - wb) * o2
    out_ref[...] = x_ref[...] + _dot(o.astype(BF16), wo_ref[...])


def _attn_merge(outs, lses, x, expand_mat, w_out, *, tm=512):
    B, S, D = x.shape
    tm = min(tm, S)
    d0, d1, d2 = ATT_DILATIONS
    grp = lambda d, w: pl.BlockSpec((None, d, tm // d, w), lambda b, i: (b, 0, i, 0))
    row = pl.BlockSpec((None, tm, D), lambda b, i: (b, i, 0))
    return pl.pallas_call(
        functools.partial(_attn_merge_kernel, tm=tm),
        grid=(B, S // tm),
        in_specs=[grp(d0, D), grp(d1, D), grp(d2, D), grp(d0, LANES), grp(d1, LANES),
                  grp(d2, LANES), row,
                  pl.BlockSpec((LANES, D), lambda b, i: (0, 0)),
                  pl.BlockSpec((D, D), lambda b, i: (0, 0))],
        out_specs=row,
        out_shape=jax.ShapeDtypeStruct((B, S, D), F32),
        scratch_shapes=[pltpu.VMEM((D // LANES, tm, LANES), F32),
                        pltpu.VMEM((D // LANES, tm, LANES), F32),
                        pltpu.VMEM((tm, LANES), F32), pltpu.VMEM((tm, LANES), F32)],
        compiler_params=_params(("parallel", "parallel")),
        name="attn_merge",
    )(*outs, *lses, x, expand_mat, w_out)


def _alibi_slopes(n):
    return tuple(float(s) for s in np.asarray(2.0 ** (-8.0 * (np.arange(n) + 1) / n), np.float32))


def _trunk(x, p):
    depth = p["norm_mix"].shape[0]
    D = x.shape[-1]
    slopes = _alibi_slopes(ATT_HEADS)
    for i in range(depth):
        a = i // 2
        nw_mix = p["norm_mix"][i][None, :]
        nw_ffn = p["norm_ffn"][i][None, :]
        if i % 2 == 0:
            proj = _norm_proj(x, nw_mix, p["hg_w_in"][a], tm=512)[:, 0]
            o_f, o_b = _hgrn_recurrence(proj, p["lbs"][i])
            x = _hgrn_out(o_f, o_b, proj, x, p["hg_norm_tiled"][a], p["hg_w_out"][a])
            x = _ffn(x, nw_ffn, p["ffn_w_gate"][a], p["ffn_w_up"][a], p["ffn_w_down"][a])
        else:
            outs, lses = [], []
            for g, dil in enumerate(ATT_DILATIONS):
                w_g = p["attn_w_in"][a][:, 3 * D * g:3 * D * (g + 1)]
                assert D == PROJ_TN
                qkv = _norm_proj(x, nw_mix, w_g, dil=dil, first_scale=ATT_Q_SCALE)
                o_g, lse_g = _attn_group(qkv, dil, slopes)
                outs.append(o_g)
                lses.append(lse_g)
            x = _attn_merge(outs, lses, x, p["expand_mat"], p["attn_w_out"][a])
            final_nw = p["norm_out"][None, :] if i == depth - 1 else None
            x = _moe(x, nw_ffn, p["moe_router_hi"][a], p["moe_router_lo"][a], p["rank_tri"],
                     p["moe_w_gate"][a], p["moe_w_up"][a], p["moe_w_down"][a], final_nw)
    return x


def kernel(x_prompt, x_sample, norm_mix, norm_ffn, norm_out, hg_w_in, hg_lower_bounds, hg_norm,
           hg_w_out, attn_w_in, attn_w_out, ffn_w_gate, ffn_w_up, ffn_w_down, moe_router,
           moe_w_gate, moe_w_up, moe_w_down):
    D = x_prompt.shape[-1]
    assert norm_mix.shape[0] % 2 == 0, "trunk ends on an attention/MoE layer"
    pr = jax.nn.softmax(hg_lower_bounds.astype(F32), axis=0)
    head_of_lane = np.arange(D) // ATT_HEAD_DIM
    expand_mat = jnp.asarray(np.arange(LANES)[:, None] == head_of_lane[None, :], BF16)
    router = jnp.pad(moe_router.astype(F32), ((0, 0), (0, 0), (0, LANES - N_EXPERTS)))
    router_hi = router.astype(BF16)
    p = dict(
        norm_mix=norm_mix, norm_ffn=norm_ffn, norm_out=norm_out,
        lbs=jnp.cumsum(pr, axis=0) - pr[0],
        hg_w_in=hg_w_in.astype(BF16), hg_w_out=hg_w_out.astype(BF16),
        hg_norm_tiled=jnp.tile(hg_norm.astype(F32), (1, D // HG_HEAD_DIM))[:, None, :],
        attn_w_in=attn_w_in.astype(BF16), attn_w_out=attn_w_out.astype(BF16),
        ffn_w_gate=ffn_w_gate.astype(BF16), ffn_w_up=ffn_w_up.astype(BF16),
        ffn_w_down=ffn_w_down.astype(BF16),
        moe_router_hi=router_hi,
        moe_router_lo=(router - router_hi.astype(F32)).astype(BF16),
        moe_w_gate=moe_w_gate.astype(BF16), moe_w_up=moe_w_up.astype(BF16),
        moe_w_down=moe_w_down.astype(BF16),
        expand_mat=expand_mat,
        rank_tri=jnp.asarray(np.tri(ROUTER_TM, k=-1), BF16),
    )
    return (_trunk(x_prompt, p), _trunk(x_sample, p))
```

```python
import functools

import jax
import jax.numpy as jnp
import numpy as np
from jax import lax
from jax.experimental import pallas as pl
from jax.experimental.pallas import tpu as pltpu

F32 = jnp.float32
BF16 = jnp.bfloat16

EPS = 1e-6
HG_HEAD_DIM = 128
HG_CHUNK = 64
HG_SUB = 32
HG_EXP_CLAMP = 80.0
HG_UNROLL = 16
ATT_HEADS = 16
ATT_HEAD_DIM = 64
ATT_RADIUS = 64
ATT_DILATIONS = (1, 4, 16)
ATT_QB = 128
ATT_KW = ATT_QB + 2 * ATT_RADIUS
LOG2E = 1.4426950408889634
LN2 = 0.6931471805599453
ATT_Q_SCALE = ATT_HEAD_DIM ** -0.5 * LOG2E
N_EXPERTS = 8
LANES = 128
VMEM_LIMIT = 56 * 1024 * 1024


def _params(sem):
    return pltpu.CompilerParams(dimension_semantics=sem, vmem_limit_bytes=VMEM_LIMIT)


def _rms(x, w):
    return (x * lax.rsqrt(jnp.mean(x * x, axis=-1, keepdims=True) + EPS)) * w


def _sigmoid(x):
    return 1.0 / (1.0 + jnp.exp(-x))


def _dot(a, b):
    return jnp.dot(a, b, preferred_element_type=F32)


def _dot_nt(a, b):
    return lax.dot_general(a, b, (((1,), (1,)), ((), ())), preferred_element_type=F32)


def _dot_tn(a, b):
    return lax.dot_general(a, b, (((0,), (0,)), ((), ())), preferred_element_type=F32)


PROJ_TN = 1024


def _norm_proj_kernel(x_ref, nw_ref, w_ref, o_ref, h_ref, *xs_ref, dil, tm, first_scale):
    rows = tm // dil
    nw = nw_ref[...]
    if dil == 1:
        h_ref[...] = _rms(x_ref[...], nw).astype(BF16)
    else:
        xs = xs_ref[0]
        n_chunks = xs.shape[0]
        for c in range(n_chunks):
            xs[c] = x_ref[:, c * LANES:(c + 1) * LANES]
        for r in range(dil):
            xr = jnp.concatenate(
                [xs[c, pl.ds(r, rows, stride=dil), :] for c in range(n_chunks)], axis=1)
            h_ref[r * rows:(r + 1) * rows, :] = _rms(xr, nw).astype(BF16)

    h = h_ref[...]
    for j in range(w_ref.shape[1] // PROJ_TN):
        cols = slice(j * PROJ_TN, (j + 1) * PROJ_TN)
        acc = _dot(h, w_ref[:, cols])
        if j == 0 and first_scale is not None:
            acc = acc * first_scale
        for r in range(dil):
            o_ref[r, :, cols] = acc[r * rows:(r + 1) * rows].astype(o_ref.dtype)


def _norm_proj(x, nw, w, *, dil=1, tm=1024, first_scale=None, out_dtype=BF16):
    B, S, D = x.shape
    N = w.shape[1]
    tm = min(tm, S)
    kern = functools.partial(_norm_proj_kernel, dil=dil, tm=tm, first_scale=first_scale)
    return pl.pallas_call(
        kern,
        grid=(B, S // tm),
        in_specs=[
            pl.BlockSpec((None, tm, D), lambda b, i: (b, i, 0)),
            pl.BlockSpec((1, D), lambda b, i: (0, 0)),
            pl.BlockSpec((D, N), lambda b, i: (0, 0)),
        ],
        out_specs=pl.BlockSpec((None, dil, tm // dil, N), lambda b, i: (b, 0, i, 0)),
        out_shape=jax.ShapeDtypeStruct((B, dil, S // dil, N), out_dtype),
        scratch_shapes=[pltpu.VMEM((tm, D), BF16)]
        + ([pltpu.VMEM((D // LANES, tm, LANES), F32)] if dil > 1 else []),
        compiler_params=_params(("parallel", "parallel")),
        name="norm_proj",
    )(x, nw, w)


def _hgrn_gates(q, f_raw, lb, tri, rev):
    q = q.astype(F32)
    q = q * _sigmoid(q)
    f = lb + (1.0 - lb) * _sigmoid(f_raw.astype(F32))
    g = jnp.log(f)
    g1 = g.astype(BF16)
    d1 = g - g1.astype(F32)
    g2 = d1.astype(BF16)
    g3 = (d1 - g2.astype(F32)).astype(BF16)
    b = _dot(tri, g1) + _dot(tri, g2) + _dot(tri, g3)
    return q, 1.0 - f, b


def _hgrn_scores(q, k, v, b, rev):
    C, SUB = HG_CHUNK, HG_SUB
    b_tot = b[0:1, :] if rev else b[C - 1:C, :]
    q_inter = (q * jnp.exp(b)).astype(BF16)
    kk = (k * jnp.exp(b_tot - b)).astype(BF16)
    st_inc = _dot_tn(v, kk)
    scores = []
    for i in range(C // SUB):
        lo, hi = i * SUB, (i + 1) * SUB
        if rev:
            klo, khi, mid = lo, C, lo + SUB // 2
        else:
            klo, khi, mid = 0, hi, lo + SUB // 2 - 1
        mu = b[mid:mid + 1, :]
        qi = (q[lo:hi] * jnp.exp(jnp.minimum(b[lo:hi] - mu, HG_EXP_CLAMP))).astype(BF16)
        ki = (k[klo:khi] * jnp.exp(jnp.minimum(mu - b[klo:khi], HG_EXP_CLAMP))).astype(BF16)
        scores.append(_dot_nt(qi, ki))
    return q_inter, jnp.exp(b_tot), st_inc, scores


def _hgrn_outputs(scores, v, q_inter, st, rev):
    C, SUB = HG_CHUNK, HG_SUB
    outs = []
    for i, a in enumerate(scores):
        lo = i * SUB
        klo, khi = (lo, C) if rev else (0, lo + SUB)
        t_idx = lo + lax.broadcasted_iota(jnp.int32, a.shape, 0)
        s_idx = klo + lax.broadcasted_iota(jnp.int32, a.shape, 1)
        a = jnp.where(s_idx >= t_idx if rev else s_idx <= t_idx, a, 0.0)
        outs.append(_dot(a.astype(BF16), v[klo:khi]))
    return _dot_nt(q_inter, st.astype(BF16)) + jnp.concatenate(outs, axis=0)


def _hgrn_kernel(qf_ref, ff_ref, vf_ref, qb_ref, fb_ref, vb_ref, lb_ref, of_ref, ob_ref,
                 sf_ref, sb_ref, *, T):
    C, U = HG_CHUNK, HG_UNROLL
    span = C * U

    @pl.when(pl.program_id(2) == 0)
    def _():
        sf_ref[...] = jnp.zeros_like(sf_ref)
        sb_ref[...] = jnp.zeros_like(sb_ref)

    row = lax.broadcasted_iota(jnp.int32, (C, C), 0)
    col = lax.broadcasted_iota(jnp.int32, (C, C), 1)
    tri_f = jnp.where(col <= row, 1.0, 0.0).astype(BF16)
    tri_b = jnp.where(col >= row, 1.0, 0.0).astype(BF16)
    lbf = lb_ref[0:1, :]
    lbb = lb_ref[1:2, :]
    n_spans = T // span

    def body(it, carry):
        r_f = pl.multiple_of(it * span, span)
        r_b = pl.multiple_of((n_spans - 1 - it) * span, span)
        rows_f, rows_b = pl.ds(r_f, span), pl.ds(r_b, span)
        qf, ff, vf = qf_ref[rows_f, :], ff_ref[rows_f, :], vf_ref[rows_f, :]
        qb, fb, vb = qb_ref[rows_b, :], fb_ref[rows_b, :], vb_ref[rows_b, :]
        chains = []
        for u in range(U):
            cf = slice(u * C, (u + 1) * C)
            chains.append((qf[cf], ff[cf], vf[cf], lbf, tri_f, False))
            cb = slice((U - 1 - u) * C, (U - u) * C)
            chains.append((qb[cb], fb[cb], vb[cb], lbb, tri_b, True))
        gates = [_hgrn_gates(q, f, lb, tri, rev) for q, f, _, lb, tri, rev in chains]
        mids = [_hgrn_scores(q, k, ch[2], b, ch[5]) for (q, k, b), ch in zip(gates, chains)]
        st = {False: sf_ref[...], True: sb_ref[...]}
        outs = {False: [], True: []}
        for (q_inter, decay, st_inc, scores), ch in zip(mids, chains):
            rev = ch[5]
            outs[rev].append(_hgrn_outputs(scores, ch[2], q_inter, st[rev], rev))
            st[rev] = st[rev] * decay + st_inc
        of_ref[rows_f, :] = jnp.concatenate(outs[False], axis=0).astype(of_ref.dtype)
        ob_ref[rows_b, :] = jnp.concatenate(outs[True][::-1], axis=0).astype(ob_ref.dtype)
        sf_ref[...] = st[False]
        sb_ref[...] = st[True]
        return carry

    lax.fori_loop(0, n_spans, body, 0)


def _hgrn_recurrence(proj, lb, *, T=2048):
    B, S, D5 = proj.shape
    D = D5 // 5
    H = D // HG_HEAD_DIM
    T = min(T, S)
    nT = S // T
    blk = (None, T, HG_HEAD_DIM)
    fwd = lambda off: pl.BlockSpec(blk, lambda b, h, t: (b, t, off + h))
    bwd = lambda off: pl.BlockSpec(blk, lambda b, h, t: (b, nT - 1 - t, off + h))
    out_shape = jax.ShapeDtypeStruct((B, S, D), BF16)
    return pl.pallas_call(
        functools.partial(_hgrn_kernel, T=T),
        grid=(B, H, nT),
        in_specs=[fwd(0), fwd(H), fwd(3 * H), bwd(0), bwd(2 * H), bwd(3 * H),
                  pl.BlockSpec((2, HG_HEAD_DIM), lambda b, h, t: (0, h))],
        out_specs=[pl.BlockSpec(blk, lambda b, h, t: (b, t, h)),
                   pl.BlockSpec(blk, lambda b, h, t: (b, nT - 1 - t, h))],
        out_shape=[out_shape, out_shape],
        scratch_shapes=[pltpu.VMEM((HG_HEAD_DIM, HG_HEAD_DIM), F32),
                        pltpu.VMEM((HG_HEAD_DIM, HG_HEAD_DIM), F32)],
        compiler_params=_params(("parallel", "parallel", "arbitrary")),
        name="hgrn_recurrence",
    )(proj, proj, proj, proj, proj, proj, lb)


def _swiglu_step(h_ref, wg_ref, wu_ref, wd_ref, acc_ref):
    h = h_ref[...]
    g = _dot(h, wg_ref[...])
    u = _dot(h, wu_ref[...])
    a = (g * _sigmoid(g)) * u
    acc_ref[...] += _dot(a.astype(BF16), wd_ref[...])


def _hgrn_out_ffn_kernel(of_ref, ob_ref, gate_ref, x_ref, wn_ref, wo_ref, nw_ref, wg_ref, wu_ref,
                         wd_ref, o_ref, x1_ref, h_ref, acc_ref):
    j = pl.program_id(2)

    @pl.when(j == 0)
    def _():
        o = of_ref[...].astype(F32) + ob_ref[...].astype(F32)
        parts = []
        for h in range(o.shape[-1] // HG_HEAD_DIM):
            oh = o[:, h * HG_HEAD_DIM:(h + 1) * HG_HEAD_DIM]
            parts.append(oh * lax.rsqrt(jnp.mean(oh * oh, axis=-1, keepdims=True) + EPS))
        gate = gate_ref[...].astype(F32)
        y = (jnp.concatenate(parts, axis=-1) * wn_ref[...]) * (gate * _sigmoid(gate))
        x1 = x_ref[...] + _dot(y.astype(BF16), wo_ref[...])
        x1_ref[...] = x1
        h_ref[...] = _rms(x1, nw_ref[...]).astype(BF16)
        acc_ref[...] = jnp.zeros_like(acc_ref)

    _swiglu_step(h_ref, wg_ref, wu_ref, wd_ref, acc_ref)

    @pl.when(j == pl.num_programs(2) - 1)
    def _():
        o_ref[...] = x1_ref[...] + acc_ref[...]


def _hgrn_out_ffn(o_f, o_b, proj, x, wn_tiled, w_out, nw, w_gate, w_up, w_down,
                  *, tm=512, tf=1792):
    B, S, D = x.shape
    F = w_gate.shape[-1]
    tm = min(tm, S)
    row = pl.BlockSpec((None, tm, D), lambda b, i, j: (b, i, 0))
    vec = pl.BlockSpec((1, D), lambda b, i, j: (0, 0))
    return pl.pallas_call(
        _hgrn_out_ffn_kernel,
        grid=(B, S // tm, F // tf),
        in_specs=[row, row,
                  pl.BlockSpec((None, tm, D), lambda b, i, j: (b, i, 4)),
                  row, vec,
                  pl.BlockSpec((D, D), lambda b, i, j: (0, 0)),
                  vec,
                  pl.BlockSpec((D, tf), lambda b, i, j: (0, j)),
                  pl.BlockSpec((D, tf), lambda b, i, j: (0, j)),
                  pl.BlockSpec((tf, D), lambda b, i, j: (j, 0))],
        out_specs=row,
        out_shape=jax.ShapeDtypeStruct((B, S, D), F32),
        scratch_shapes=[pltpu.VMEM((tm, D), F32), pltpu.VMEM((tm, D), BF16),
                        pltpu.VMEM((tm, D), F32)],
        compiler_params=_params(("parallel", "parallel", "arbitrary")),
        name="hgrn_out_ffn",
    )(o_f, o_b, proj, x, wn_tiled, w_out, nw, w_gate, w_up, w_down)


MOE_RB = 512
MOE_TF = 1792
ROUTER_TM = 512
DISPATCH_TM = 512
COMBINE_TM = 256


def _router_kernel(x_ref, nw_ref, whi_ref, wlo_ref, tri_ref, w_ref, i_ref, cnt_ref, run_ref):
    @pl.when(pl.program_id(0) == 0)
    def _():
        run_ref[...] = jnp.zeros_like(run_ref)

    h = _rms(x_ref[...], nw_ref[...])
    h_hi = h.astype(BF16)
    h_lo = (h - h_hi.astype(F32)).astype(BF16)
    whi = whi_ref[...]
    logits = _dot(h_hi, whi) + _dot(h_hi, wlo_ref[...]) + _dot(h_lo, whi)

    tm = h.shape[0]
    lane = lax.broadcasted_iota(jnp.int32, (tm, LANES), 1)
    lane_f = lane.astype(F32)

    def top(vals):
        m = jnp.max(vals, axis=-1, keepdims=True)
        idx = jnp.min(jnp.where(vals == m, lane_f, float(LANES)), axis=-1, keepdims=True)
        return m, idx.astype(jnp.int32)

    logits = jnp.where(lane < N_EXPERTS, logits, -jnp.inf)
    m1, i1 = top(logits)
    m2, i2 = top(jnp.where(lane == i1, -jnp.inf, logits))
    e2 = jnp.exp(m2 - m1)
    w1 = 1.0 / (1.0 + e2)
    w2 = e2 / (1.0 + e2)

    oh1 = jnp.where(lane == i1, 1.0, 0.0)
    oh2 = jnp.where(lane == i2, 1.0, 0.0)
    oh = oh1 + oh2
    before = _dot(tri_ref[...], oh.astype(BF16)) + run_ref[...]
    r1 = jnp.sum(oh1 * before, axis=-1, keepdims=True).astype(jnp.int32)
    r2 = jnp.sum(oh2 * before, axis=-1, keepdims=True).astype(jnp.int32)
    run_ref[...] += jnp.sum(oh, axis=0, keepdims=True)
    cnt_ref[...] = run_ref[...].astype(jnp.int32)
    w_ref[...] = jnp.where(lane == 0, w1, jnp.where(lane == 1, w2, 0.0))
    i_ref[...] = jnp.where(lane == 0, i1, jnp.where(lane == 1, i2,
                           jnp.where(lane == 2, r1, jnp.where(lane == 3, r2, 0))))


def _router(x2, nw, w_hi, w_lo, tri, *, tm=ROUTER_TM):
    N, D = x2.shape
    row = lambda w: pl.BlockSpec((tm, w), lambda i: (i, 0))
    return pl.pallas_call(
        _router_kernel,
        grid=(N // tm,),
        in_specs=[row(D),
                  pl.BlockSpec((1, D), lambda i: (0, 0)),
                  pl.BlockSpec((D, LANES), lambda i: (0, 0)),
                  pl.BlockSpec((D, LANES), lambda i: (0, 0)),
                  pl.BlockSpec((tm, tm), lambda i: (0, 0))],
        out_specs=[row(LANES), row(LANES), pl.BlockSpec((1, LANES), lambda i: (0, 0))],
        out_shape=[jax.ShapeDtypeStruct((N, LANES), F32),
                   jax.ShapeDtypeStruct((N, LANES), jnp.int32),
                   jax.ShapeDtypeStruct((1, LANES), jnp.int32)],
        scratch_shapes=[pltpu.VMEM((1, LANES), F32)],
        compiler_params=_params(("arbitrary",)),
        name="moe_router",
    )(x2, nw, w_hi, w_lo, tri)


def _row_copy(src_hbm, src_row, dst_ref, dst_row, sem):
    return pltpu.make_async_copy(src_hbm.at[pl.ds(src_row, 1)], dst_ref.at[pl.ds(dst_row, 1)], sem)


def _dispatch_kernel(zero_blk_ref, n_zero_ref, dst_ref, x_ref, xs_hbm, zero_ref, sem, zsem,
                     *, tm):
    @pl.when(pl.program_id(0) == 0)
    def _():
        zero_ref[...] = jnp.zeros_like(zero_ref)
        rb = zero_ref.shape[0]

        def fill(c, carry):
            row = pl.multiple_of(zero_blk_ref[c] * rb, rb)
            copy = pltpu.make_async_copy(zero_ref, xs_hbm.at[pl.ds(row, rb)], zsem)
            copy.start()
            copy.wait()
            return carry

        lax.fori_loop(0, n_zero_ref[0], fill, 0)

    def issue(t, carry):
        _row_copy(x_ref, t, xs_hbm, dst_ref[0, 2 * t], sem).start()
        _row_copy(x_ref, t, xs_hbm, dst_ref[0, 2 * t + 1], sem).start()
        return carry

    lax.fori_loop(0, tm, issue, 0, unroll=8)
    for _ in range(2):
        pltpu.make_async_copy(x_ref, xs_hbm.at[pl.ds(0, tm)], sem).wait()


def _dispatch(x2, dst, zero_blk, n_zero, n_rows, *, tm=DISPATCH_TM):
    N, D = x2.shape
    return pl.pallas_call(
        functools.partial(_dispatch_kernel, tm=tm),
        grid_spec=pltpu.PrefetchScalarGridSpec(
            num_scalar_prefetch=2,
            grid=(N // tm,),
            in_specs=[pl.BlockSpec((None, 1, 2 * tm), lambda i, ps, pn: (i, 0, 0),
                                   memory_space=pltpu.SMEM),
                      pl.BlockSpec((tm, D), lambda i, ps, pn: (i, 0))],
            out_specs=pl.BlockSpec(memory_space=pl.ANY),
            scratch_shapes=[pltpu.VMEM((MOE_RB, D), F32),
                            pltpu.SemaphoreType.DMA(()), pltpu.SemaphoreType.DMA(())]),
        out_shape=jax.ShapeDtypeStruct((n_rows, D), F32),
        compiler_params=_params(("arbitrary",)),
        name="moe_dispatch",
    )(zero_blk, n_zero, dst.reshape(N // tm, 1, 2 * tm), x2)


def _expert_kernel(be_ref, nb_ref, x_ref, nw_ref, wg_ref, wu_ref, wd_ref, o_ref, h_ref, acc_ref):
    del be_ref
    j = pl.program_id(1)
    used = pl.program_id(0) < nb_ref[0]

    @pl.when(jnp.logical_and(jnp.logical_not(used), j == 0))
    def _():
        o_ref[...] = jnp.zeros_like(o_ref)

    @pl.when(used)
    def _():
        @pl.when(j == 0)
        def _():
            h_ref[...] = _rms(x_ref[...], nw_ref[...]).astype(BF16)
            acc_ref[...] = jnp.zeros_like(acc_ref)

        _swiglu_step(h_ref, wg_ref, wu_ref, wd_ref, acc_ref)

        @pl.when(j == pl.num_programs(1) - 1)
        def _():
            o_ref[...] = acc_ref[...]


def _experts(xs, nw, blk_expert, n_blk, w_gate, w_up, w_down, *, rb=MOE_RB, tf=MOE_TF):
    R, D = xs.shape
    F = w_gate.shape[-1]
    nj = F // tf
    used = lambda r, nb: jnp.minimum(r, nb[0] - 1)
    jj = lambda r, j, nb: jnp.where(r < nb[0], j, nj - 1)
    row = pl.BlockSpec((rb, D), lambda r, j, be, nb: (used(r, nb), 0))
    w_in = pl.BlockSpec((None, D, tf), lambda r, j, be, nb: (be[used(r, nb)], 0, jj(r, j, nb)))
    w_out = pl.BlockSpec((None, tf, D), lambda r, j, be, nb: (be[used(r, nb)], jj(r, j, nb), 0))
    return pl.pallas_call(
        _expert_kernel,
        grid_spec=pltpu.PrefetchScalarGridSpec(
            num_scalar_prefetch=2,
            grid=(R // rb, nj),
            in_specs=[row, pl.BlockSpec((1, D), lambda r, j, be, nb: (0, 0)), w_in, w_in, w_out],
            out_specs=pl.BlockSpec((rb, D), lambda r, j, be, nb: (r, 0)),
            scratch_shapes=[pltpu.VMEM((rb, D), BF16), pltpu.VMEM((rb, D), F32)]),
        out_shape=jax.ShapeDtypeStruct((R, D), F32),
        compiler_params=_params(("arbitrary", "arbitrary")),
        name="moe_experts",
    )(blk_expert, n_blk, xs, nw, w_gate, w_up, w_down)


def _combine_kernel(*refs, tm, final_norm):
    if final_norm:
        cur_ref, nxt_ref, w_ref, x_ref, y_hbm, fnw_ref, o_ref, ybuf, sem = refs
    else:
        cur_ref, nxt_ref, w_ref, x_ref, y_hbm, o_ref, ybuf, sem = refs
    i = pl.program_id(0)
    slot = i % 2

    def issue(idx_ref, s):
        def body(t, carry):
            _row_copy(y_hbm, idx_ref[0, 2 * t], ybuf.at[s], t, sem.at[s]).start()
            _row_copy(y_hbm, idx_ref[0, 2 * t + 1], ybuf.at[s], tm + t, sem.at[s]).start()
            return carry
        lax.fori_loop(0, tm, body, 0, unroll=8)

    @pl.when(i == 0)
    def _():
        issue(cur_ref, 0)

    @pl.when(i + 1 < pl.num_programs(0))
    def _():
        issue(nxt_ref, 1 - slot)

    pltpu.make_async_copy(y_hbm.at[pl.ds(0, 2 * tm)], ybuf.at[slot], sem.at[slot]).wait()
    w = w_ref[...]
    y = ybuf[slot]
    out = x_ref[...] + w[:, 0:1] * y[0:tm] + w[:, 1:2] * y[tm:2 * tm]
    if final_norm:
        out = _rms(out, fnw_ref[...])
    o_ref[...] = out


def _combine(x2, y, dst, wts, final_nw, *, tm=COMBINE_TM):
    N, D = x2.shape
    nt = N // tm
    idx = dst.reshape(nt, 1, 2 * tm)
    smem = lambda f: pl.BlockSpec((None, 1, 2 * tm), f, memory_space=pltpu.SMEM)
    in_specs = [smem(lambda i: (i, 0, 0)),
                smem(lambda i: (jnp.minimum(i + 1, nt - 1), 0, 0)),
                pl.BlockSpec((tm, LANES), lambda i: (i, 0)),
                pl.BlockSpec((tm, D), lambda i: (i, 0)),
                pl.BlockSpec(memory_space=pl.ANY)]
    args = [idx, idx, wts, x2, y]
    if final_nw is not None:
        in_specs.append(pl.BlockSpec((1, D), lambda i: (0, 0)))
        args.append(final_nw)
    return pl.pallas_call(
        functools.partial(_combine_kernel, tm=tm, final_norm=final_nw is not None),
        grid=(nt,),
        in_specs=in_specs,
        out_specs=pl.BlockSpec((tm, D), lambda i: (i, 0)),
        out_shape=jax.ShapeDtypeStruct((N, D), F32),
        scratch_shapes=[pltpu.VMEM((2, 2 * tm, D), F32), pltpu.SemaphoreType.DMA((2,))],
        compiler_params=_params(("arbitrary",)),
        name="moe_combine",
    )(*args)


def _moe(x, nw, w_router_hi, w_router_lo, tri, w_gate, w_up, w_down, final_nw):
    B, S, D = x.shape
    N = B * S
    rb = MOE_RB
    x2 = x.reshape(N, D)
    wts, meta, cnt = _router(x2, nw, w_router_hi, w_router_lo, tri)
    counts = cnt[0, :N_EXPERTS]
    group = ((counts + rb - 1) // rb) * rb
    ends = jnp.cumsum(group)
    offs = ends - group
    dst = jnp.stack([offs[meta[:, 0]] + meta[:, 2], offs[meta[:, 1]] + meta[:, 3]], axis=-1)
    n_blocks = (2 * N) // rb + N_EXPERTS
    blk_expert = jnp.minimum(
        jnp.searchsorted(ends // rb, jnp.arange(n_blocks, dtype=jnp.int32), side="right"),
        N_EXPERTS - 1).astype(jnp.int32)
    n_blk = (ends[-1:] // rb).astype(jnp.int32)
    cand = jnp.concatenate([ends // rb - 1, n_blk + jnp.arange(N_EXPERTS, dtype=jnp.int32)])
    keep = jnp.concatenate([group > counts, n_blk + jnp.arange(N_EXPERTS) < n_blocks])
    zero_blk = cand[jnp.argsort(jnp.logical_not(keep), stable=True)].astype(jnp.int32)
    n_zero = jnp.sum(keep, keepdims=True).astype(jnp.int32)
    xs = _dispatch(x2, dst, zero_blk, n_zero, n_blocks * rb)
    y = _experts(xs, nw, blk_expert, n_blk, w_gate, w_up, w_down)
    return _combine(x2, y, dst, wts, final_nw).reshape(B, S, D)


def _attn_kernel(q_ref, k_ref, v_ref, kp_ref, kn_ref, vp_ref, vn_ref, bias_ref, o_ref, lse_ref,
                 kw_ref, vw_ref, *, L, TQ):
    R, QB, KW = ATT_RADIUS, ATT_QB, ATT_KW
    n = pl.program_id(2)
    kw_ref[0:R, :] = kp_ref[...]
    kw_ref[R:R + TQ, :] = k_ref[...]
    kw_ref[R + TQ:, :] = kn_ref[...]
    vw_ref[0:R, :] = vp_ref[...]
    vw_ref[R:R + TQ, :] = v_ref[...]
    vw_ref[R + TQ:, :] = vn_ref[...]

    kcol = lax.broadcasted_iota(jnp.int32, (QB, KW), 1)
    lane = lax.broadcasted_iota(jnp.int32, (QB, LANES), 1)
    lo_half = lane < ATT_HEAD_DIM

    def sub_block(i, at_edge):
        r0 = pl.multiple_of(i * QB, QB)
        if at_edge:
            kpos = n * TQ + i * QB - R + kcol
            in_seq = (kpos >= 0) & (kpos < L)
        scores = []
        for h in range(ATT_HEADS):
            cs = slice(h // 2 * LANES, (h // 2 + 1) * LANES)
            q2 = q_ref[pl.ds(r0, QB), cs]
            qm = jnp.where(lo_half if h % 2 == 0 else jnp.logical_not(lo_half), q2, 0.0)
            scores.append(_dot_nt(qm.astype(BF16), kw_ref[pl.ds(r0, KW), cs]) + bias_ref[h])
        lse_tile = jnp.zeros((QB, LANES), F32)
        probs = []
        for h, s in enumerate(scores):
            if at_edge:
                s = jnp.where(in_seq, s, -jnp.inf)
            m = jnp.max(s, axis=-1, keepdims=True)
            pe = jnp.exp2(s - m)
            den = jnp.sum(pe, axis=-1, keepdims=True)
            probs.append((pe.astype(BF16), den))
            lse_tile = jnp.where(lane == h, LN2 * (m + jnp.log2(den)), lse_tile)
        lse_ref[pl.ds(r0, QB), :] = lse_tile
        outs = []
        for h, (pe, den) in enumerate(probs):
            cs = slice(h // 2 * LANES, (h // 2 + 1) * LANES)
            outs.append(_dot(pe, vw_ref[pl.ds(r0, KW), cs]) / den)
            if h % 2 == 1:
                o_ref[pl.ds(r0, QB), cs] = jnp.where(
                    lo_half, outs[h - 1], outs[h]).astype(o_ref.dtype)

    def sub(i, carry):
        q0 = n * TQ + i * QB
        at_edge = jnp.logical_or(q0 == 0, q0 + QB == L)
        pl.when(at_edge)(lambda: sub_block(i, True))
        pl.when(jnp.logical_not(at_edge))(lambda: sub_block(i, False))
        return carry

    lax.fori_loop(0, TQ // QB, sub, 0)


def _attn_bias(dil, slopes):
    rel = np.arange(ATT_KW)[None, :] - ATT_RADIUS - np.arange(ATT_QB)[:, None]
    arel = np.abs(rel).astype(np.float32)
    bias = -np.asarray(slopes, np.float32)[:, None, None] * (np.float32(dil) * arel)[None]
    bias = np.where((arel <= ATT_RADIUS)[None], bias * np.float32(LOG2E), -np.inf)
    return jnp.asarray(bias, F32)


def _attn_group(qkv, dil, slopes, *, TQ=512):
    B, _, L, D3 = qkv.shape
    D = D3 // 3
    R = ATT_RADIUS
    TQ = min(TQ, L)
    nq = L // TQ
    hb = TQ // R
    last_hb = L // R - 1
    main = lambda c: pl.BlockSpec((None, None, TQ, D), lambda b, r, n: (b, r, n, c))
    prev = lambda c: pl.BlockSpec((None, None, R, D),
                                  lambda b, r, n: (b, r, jnp.maximum(n * hb - 1, 0), c))
    nxt = lambda c: pl.BlockSpec((None, None, R, D),
                                 lambda b, r, n: (b, r, jnp.minimum((n + 1) * hb, last_hb), c))
    kern = functools.partial(_attn_kernel, L=L, TQ=TQ)
    return pl.pallas_call(
        kern,
        grid=(B, dil, nq),
        in_specs=[main(0), main(1), main(2), prev(1), nxt(1), prev(2), nxt(2),
                  pl.BlockSpec((ATT_HEADS, ATT_QB, ATT_KW), lambda b, r, n: (0, 0, 0))],
        out_specs=[pl.BlockSpec((None, None, TQ, D), lambda b, r, n: (b, r, n, 0)),
                   pl.BlockSpec((None, None, TQ, LANES), lambda b, r, n: (b, r, n, 0))],
        out_shape=[jax.ShapeDtypeStruct((B, dil, L, D), BF16),
                   jax.ShapeDtypeStruct((B, dil, L, LANES), F32)],
        scratch_shapes=[pltpu.VMEM((TQ + 2 * R, D), BF16), pltpu.VMEM((TQ + 2 * R, D), BF16)],
        compiler_params=_params(("parallel", "parallel", "parallel")),
        name="dilated_attn",
    )(qkv, qkv, qkv, qkv, qkv, qkv, qkv, _attn_bias(dil, slopes))


def _attn_merge_kernel(o0_ref, o1_ref, o2_ref, l0_ref, l1_ref, l2_ref, x_ref, ex_ref, wo_ref,
                       out_ref, s1_ref, s2_ref, t1_ref, t2_ref, *, tm):
    n_chunks = s1_ref.shape[0]

    def interleave(src_ref, lse_src_ref, dst_ref, lse_dst_ref, dil):
        rows = tm // dil
        for r in range(dil):
            lse_dst_ref[pl.ds(r, rows, stride=dil), :] = lse_src_ref[r]
            for c in range(n_chunks):
                dst_ref[c, pl.ds(r, rows, stride=dil), :] = (
                    src_ref[r, :, c * LANES:(c + 1) * LANES].astype(F32))
        return jnp.concatenate([dst_ref[c] for c in range(n_chunks)], axis=1)

    o1 = interleave(o1_ref, l1_ref, s1_ref, t1_ref, ATT_DILATIONS[1])
    o2 = interleave(o2_ref, l2_ref, s2_ref, t2_ref, ATT_DILATIONS[2])
    la, lb, lc = l0_ref[0], t1_ref[...], t2_ref[...]
    m = jnp.maximum(jnp.maximum(la, lb), lc)
    ea, eb, ec = jnp.exp(la - m), jnp.exp(lb - m), jnp.exp(lc - m)
    den = ea + eb + ec
    ex = ex_ref[...]

    def expand(w):
        return _dot(w.astype(BF16), ex)

    wa, wb = expand(ea / den), expand(eb / den)
    o = wa * o0_ref[0].astype(F32) + wb * o1 + (1.0 - wa - wb) * o2
    out_ref[...] = x_ref[...] + _dot(o.astype(BF16), wo_ref[...])


def _attn_merge(outs, lses, x, expand_mat, w_out, *, tm=512):
    B, S, D = x.shape
    tm = min(tm, S)
    d0, d1, d2 = ATT_DILATIONS
    grp = lambda d, w: pl.BlockSpec((None, d, tm // d, w), lambda b, i: (b, 0, i, 0))
    row = pl.BlockSpec((None, tm, D), lambda b, i: (b, i, 0))
    return pl.pallas_call(
        functools.partial(_attn_merge_kernel, tm=tm),
        grid=(B, S // tm),
        in_specs=[grp(d0, D), grp(d1, D), grp(d2, D), grp(d0, LANES), grp(d1, LANES),
                  grp(d2, LANES), row,
                  pl.BlockSpec((LANES, D), lambda b, i: (0, 0)),
                  pl.BlockSpec((D, D), lambda b, i: (0, 0))],
        out_specs=row,
        out_shape=jax.ShapeDtypeStruct((B, S, D), F32),
        scratch_shapes=[pltpu.VMEM((D // LANES, tm, LANES), F32),
                        pltpu.VMEM((D // LANES, tm, LANES), F32),
                        pltpu.VMEM((tm, LANES), F32), pltpu.VMEM((tm, LANES), F32)],
        compiler_params=_params(("parallel", "parallel")),
        name="attn_merge",
    )(*outs, *lses, x, expand_mat, w_out)


def _alibi_slopes(n):
    return tuple(float(s) for s in np.asarray(2.0 ** (-8.0 * (np.arange(n) + 1) / n), np.float32))


def _trunk(x, p):
    depth = p["norm_mix"].shape[0]
    D = x.shape[-1]
    slopes = _alibi_slopes(ATT_HEADS)
    for i in range(depth):
        a = i // 2
        nw_mix = p["norm_mix"][i][None, :]
        nw_ffn = p["norm_ffn"][i][None, :]
        if i % 2 == 0:
            proj = _norm_proj(x, nw_mix, p["hg_w_in"][a], tm=512)[:, 0]
            o_f, o_b = _hgrn_recurrence(proj, p["lbs"][i])
            x = _hgrn_out_ffn(o_f, o_b, proj, x, p["hg_norm_tiled"][a], p["hg_w_out"][a], nw_ffn,
                              p["ffn_w_gate"][a], p["ffn_w_up"][a], p["ffn_w_down"][a])
        else:
            outs, lses = [], []
            for g, dil in enumerate(ATT_DILATIONS):
                w_g = p["attn_w_in"][a][:, 3 * D * g:3 * D * (g + 1)]
                assert D == PROJ_TN
                qkv = _norm_proj(x, nw_mix, w_g, dil=dil, first_scale=ATT_Q_SCALE)
                o_g, lse_g = _attn_group(qkv, dil, slopes)
                outs.append(o_g)
                lses.append(lse_g)
            x = _attn_merge(outs, lses, x, p["expand_mat"], p["attn_w_out"][a])
            final_nw = p["norm_out"][None, :] if i == depth - 1 else None
            x = _moe(x, nw_ffn, p["moe_router_hi"][a], p["moe_router_lo"][a], p["rank_tri"],
                     p["moe_w_gate"][a], p["moe_w_up"][a], p["moe_w_down"][a], final_nw)
    return x


def kernel(x_prompt, x_sample, norm_mix, norm_ffn, norm_out, hg_w_in, hg_lower_bounds, hg_norm,
           hg_w_out, attn_w_in, attn_w_out, ffn_w_gate, ffn_w_up, ffn_w_down, moe_router,
           moe_w_gate, moe_w_up, moe_w_down):
    D = x_prompt.shape[-1]
    assert norm_mix.shape[0] % 2 == 0, "trunk ends on an attention/MoE layer"
    pr = jax.nn.softmax(hg_lower_bounds.astype(F32), axis=0)
    head_of_lane = np.arange(D) // ATT_HEAD_DIM
    expand_mat = jnp.asarray(np.arange(LANES)[:, None] == head_of_lane[None, :], BF16)
    router = jnp.pad(moe_router.astype(F32), ((0, 0), (0, 0), (0, LANES - N_EXPERTS)))
    router_hi = router.astype(BF16)
    p = dict(
        norm_mix=norm_mix, norm_ffn=norm_ffn, norm_out=norm_out,
        lbs=jnp.cumsum(pr, axis=0) - pr[0],
        hg_w_in=hg_w_in.astype(BF16), hg_w_out=hg_w_out.astype(BF16),
        hg_norm_tiled=jnp.tile(hg_norm.astype(F32), (1, D // HG_HEAD_DIM))[:, None, :],
        attn_w_in=attn_w_in.astype(BF16), attn_w_out=attn_w_out.astype(BF16),
        ffn_w_gate=ffn_w_gate.astype(BF16), ffn_w_up=ffn_w_up.astype(BF16),
        ffn_w_down=ffn_w_down.astype(BF16),
        moe_router_hi=router_hi,
        moe_router_lo=(router - router_hi.astype(F32)).astype(BF16),
        moe_w_gate=moe_w_gate.astype(BF16), moe_w_up=moe_w_up.astype(BF16),
        moe_w_down=moe_w_down.astype(BF16),
        expand_mat=expand_mat,
        rank_tri=jnp.asarray(np.tri(ROUTER_TM, k=-1), BF16),
    )
    return (_trunk(x_prompt, p), _trunk(x_sample, p))
```
